```python
import math
import jax, jax.numpy as jnp
from jax import lax
import numpy as np

D_MODEL = 1024
BATCH = 4
SEQ = 4096
DEPTH = 2
DEC_BATCH = 128
DEC_SEQ = 8
PAST_LEN = 2048
PAGE_SIZE = 128

N_EVEN = (DEPTH + 1) // 2
N_ODD = DEPTH // 2
A_WIDTH = D_MODEL // 2
A_GROUP = 16
A_GROUPS = A_WIDTH // A_GROUP
A_STATE = 64
B_WIDTH = D_MODEL // 2
B_DK = 128
B_HEADS = B_WIDTH // B_DK
B_CHUNK = 64
D_MIX_AB = A_WIDTH + B_WIDTH
AB_COLS = A_WIDTH + 4 * B_WIDTH
C_HEADS = 8
C_HEAD_DIM = D_MODEL // C_HEADS // 2
C_VDIM = 2 * C_HEAD_DIM
C_COLS = 3 * C_HEADS * C_VDIM
Q_BLOCK = 128
REL_BUCKETS = 32
REL_MAX_DIST = 128
PEER_HEADS = 8
N_KEYS = 128
N_EXPERTS = N_KEYS * N_KEYS
PEER_TOPK = 16
PEER_QDIM = 256
PEER_BLOCK = 256
PLE_DIM = 256
EPS = 1e-6

kernel_name = 'hybrid_s5_hgrn2_diffattn_peer_step'


def rmsnorm(x, g):
    xf = x.astype(jnp.float32)
    y = xf * lax.rsqrt(jnp.mean(xf * xf, axis=-1, keepdims=True) + EPS)
    return (y * g.astype(jnp.float32)).astype(x.dtype)


def _cplx_combine(e1, e2):
    a1r, a1i, b1r, b1i = e1
    a2r, a2i, b2r, b2i = e2
    return (a1r * a2r - a1i * a2i,
            a1r * a2i + a1i * a2r,
            a2r * b1r - a2i * b1i + b2r,
            a2r * b1i + a2i * b1r + b2i)


def s5_mixer(u, h0_re, h0_im, a_re, a_im, log_dt, b_re, b_im, c_re, c_im, d, glu_w, glu_b):
    f32 = jnp.float32
    Bn, T, _ = u.shape
    uf = u.astype(f32).reshape(Bn, T, A_GROUPS, A_GROUP)
    a_re = a_re.astype(f32)
    a_im = a_im.astype(f32)
    b_re = b_re.astype(f32)
    b_im = b_im.astype(f32)
    dt = jnp.exp(log_dt.astype(f32))[:, None]
    mag = jnp.exp(a_re * dt)
    lb_re = mag * jnp.cos(a_im * dt)
    lb_im = mag * jnp.sin(a_im * dt)
    den = a_re * a_re + a_im * a_im
    coef_re = ((lb_re - 1.0) * a_re + lb_im * a_im) / den
    coef_im = (lb_im * a_re - (lb_re - 1.0) * a_im) / den
    bb_re = coef_re[..., None] * b_re - coef_im[..., None] * b_im
    bb_im = coef_re[..., None] * b_im + coef_im[..., None] * b_re
    x_re = jnp.einsum('gpc,btgc->btgp', bb_re, uf)
    x_im = jnp.einsum('gpc,btgc->btgp', bb_im, uf)
    ar = jnp.broadcast_to(lb_re, x_re.shape)
    ai = jnp.broadcast_to(lb_im, x_re.shape)
    cum_re, cum_im, s_re, s_im = lax.associative_scan(_cplx_combine, (ar, ai, x_re, x_im), axis=1)
    h0r = h0_re.astype(f32)[:, None]
    h0i = h0_im.astype(f32)[:, None]
    h_re = s_re + cum_re * h0r - cum_im * h0i
    h_im = s_im + cum_re * h0i + cum_im * h0r
    y = (jnp.einsum('gcp,btgp->btgc', c_re.astype(f32), h_re)
         - jnp.einsum('gcp,btgp->btgc', c_im.astype(f32), h_im)
         + d.astype(f32).reshape(A_GROUPS, A_GROUP) * uf)
    y = jax.nn.gelu(y.reshape(Bn, T, A_WIDTH))
    out = y * jax.nn.sigmoid(y @ glu_w.astype(f32) + glu_b.astype(f32))
    return out.astype(u.dtype), h_re[:, -1].astype(h0_re.dtype), h_im[:, -1].astype(h0_im.dtype)


def hgrn2_mixer(q, fpre, i, g, S0, lb, norm_g):
    f32 = jnp.float32
    Bn, T, _ = q.shape
    heads = lambda z: z.astype(f32).reshape(Bn, T, B_HEADS, B_DK)
    lbh = lb.astype(f32).reshape(B_HEADS, B_DK)
    f = lbh + (1.0 - lbh) * jax.nn.sigmoid(heads(fpre))
    logf = jnp.log(f)
    k = 1.0 - f
    qh = heads(q)
    vh = heads(i)
    L = math.gcd(T, B_CHUNK)
    nc = T // L
    to_chunks = lambda z: z.reshape(Bn, nc, L, B_HEADS, B_DK).transpose(1, 0, 3, 2, 4)
    causal = jnp.tril(jnp.ones((L, L), dtype=bool))

    def step(S, inp):
        qc, lfc, kc, vc = inp
        b = jnp.cumsum(lfc, axis=2)
        inter = jnp.einsum('bhtd,bhde->bhte', qc * jnp.exp(b), S)
        diff = b[:, :, :, None, :] - b[:, :, None, :, :]
        decay = jnp.exp(jnp.where(causal[:, :, None], diff, -jnp.inf))
        attn = jnp.einsum('bhtd,bhsd,bhtsd->bhts', qc, kc, decay)
        o = inter + jnp.einsum('bhts,bhse->bhte', attn, vc)
        bl = b[:, :, -1]
        S = jnp.exp(bl)[..., None] * S + jnp.einsum('bhsd,bhse->bhde', kc * jnp.exp(bl[:, :, None] - b), vc)
        return S, o

    S_T, o = lax.scan(step, S0.astype(f32), (to_chunks(qh), to_chunks(logf), to_chunks(k), to_chunks(vh)))
    o = o.transpose(1, 0, 3, 2, 4).reshape(Bn, T, B_HEADS, B_DK)
    o = rmsnorm(o, norm_g) * jax.nn.silu(heads(g))
    return o.reshape(Bn, T, B_WIDTH).astype(q.dtype), S_T.astype(S0.dtype)


def even_mixer(xn, j, s5_re0, s5_im0, S0, w):
    proj = xn @ w['w_in_ab'][j]
    u = proj[..., :A_WIDTH]
    q, fpre, i, g = jnp.split(proj[..., A_WIDTH:], 4, axis=-1)
    ya, hr, hi = s5_mixer(u, s5_re0, s5_im0, w['s5_a_re'][j], w['s5_a_im'][j], w['s5_log_dt'][j],
                          w['s5_b_re'][j], w['s5_b_im'][j], w['s5_c_re'][j], w['s5_c_im'][j],
                          w['s5_d'][j], w['s5_glu_w'][j], w['s5_glu_b'][j])
    lb = jnp.cumsum(jax.nn.softmax(w['hgrn_lower_bounds'].astype(jnp.float32), axis=0), axis=0)[j]
    yb, S = hgrn2_mixer(q, fpre, i, g, S0, lb, w['hgrn_norm'][j])
    out = jnp.concatenate([ya, yb.astype(ya.dtype)], axis=-1) @ w['w_out_ab'][j]
    return out, hr, hi, S


def rel_bucket(n):
    max_exact = REL_BUCKETS // 2
    nf = jnp.maximum(n, 1).astype(jnp.float32)
    large = max_exact + (jnp.log(nf / max_exact) / math.log(REL_MAX_DIST / max_exact)
                         * (REL_BUCKETS - max_exact)).astype(jnp.int32)
    large = jnp.minimum(large, REL_BUCKETS - 1)
    return jnp.where(n < max_exact, n, large)


def diff_attention(q, q_pos, segs, lam, rel_bias):
    f32 = jnp.float32
    Bn, Tq = q.shape[:2]
    blk = math.gcd(Tq, Q_BLOCK)
    nb = Tq // blk
    scale = C_HEAD_DIM ** -0.5
    qb = jnp.moveaxis(q.reshape(Bn, nb, blk, C_HEADS, 2, C_HEAD_DIM), 1, 0)
    pb = q_pos.reshape(nb, blk)
    k_pos = jnp.concatenate([pos for _, _, pos in segs])
    split_at = np.cumsum([pos.shape[0] for _, _, pos in segs])[:-1].tolist()
    bias_tab = rel_bias.astype(f32)

    def one(args):
        qi, pi = args
        s = jnp.concatenate([jnp.einsum('bqhcd,bkhcd->bhcqk', qi, kk) for kk, _, _ in segs], axis=-1)
        s = s.astype(f32) * scale
        dist = pi[:, None] - k_pos[None, :]
        bias = jnp.take(bias_tab, rel_bucket(jnp.maximum(dist, 0)), axis=0)
        s = s + jnp.transpose(bias, (2, 0, 1))[:, None]
        s = jnp.where(dist >= 0, s, -jnp.inf)
        pr = jax.nn.softmax(s, axis=-1)
        a = pr[:, :, 0] - lam * pr[:, :, 1]
        parts = jnp.split(a, split_at, axis=-1)
        return sum(jnp.einsum('bhqk,bkhe->bqhe', ap.astype(vv.dtype), vv) for ap, (_, vv, _) in zip(parts, segs))

    o = lax.map(one, (qb, pb))
    return jnp.moveaxis(o, 0, 1).reshape(Bn, Tq, C_HEADS, C_VDIM)


def odd_mixer(xn, j, layer, cache_k, cache_v, page_table, w):
    f32 = jnp.float32
    Bn, T, _ = xn.shape
    proj = xn @ w['w_in_c'][j]
    qz, kz, vz = jnp.split(proj, 3, axis=-1)
    q = rmsnorm(qz.reshape(Bn, T, C_HEADS, 2, C_HEAD_DIM), w['c_q_norm'][j])
    k = rmsnorm(kz.reshape(Bn, T, C_HEADS, 2, C_HEAD_DIM), w['c_k_norm'][j])
    v = vz.reshape(Bn, T, C_HEADS, C_VDIM)
    if page_table is None:
        past_len = 0
        segs = []
    else:
        pk = cache_k[j, page_table]
        past_len = pk.shape[1] * pk.shape[2]
        pk = pk.reshape(Bn, past_len, C_HEADS, 2, C_HEAD_DIM)
        pv = cache_v[j, page_table].reshape(Bn, past_len, C_HEADS, C_VDIM)
        segs = [(pk, pv, jnp.arange(past_len))]
    q_pos = past_len + jnp.arange(T)
    segs = segs + [(k, v, q_pos)]
    lam_init = 0.8 - 0.6 * math.exp(-0.3 * layer)
    lam = (jnp.exp(jnp.sum(w['c_lam_q1'][j].astype(f32) * w['c_lam_k1'][j].astype(f32)))
           - jnp.exp(jnp.sum(w['c_lam_q2'][j].astype(f32) * w['c_lam_k2'][j].astype(f32))) + lam_init)
    o = diff_attention(q, q_pos, segs, lam, w['rel_bias'])
    o = rmsnorm(o, w['c_subln'][j]) * (1.0 - lam_init)
    out = o.reshape(Bn, T, C_HEADS * C_VDIM) @ w['w_out_c'][j]
    return out, k.reshape(Bn, T, C_HEADS, 2 * C_HEAD_DIM), v


def peer(xn, wq, keys, u_tab, v_tab):
    Bn, T, D = xn.shape
    N = Bn * T
    blk = math.gcd(N, PEER_BLOCK)
    xb = xn.reshape(N // blk, blk, D)

    def one(xi):
        q = (xi @ wq).reshape(blk, PEER_HEADS, 2, PEER_QDIM // 2)
        s = jnp.einsum('nhcd,hckd->nhck', q, keys).astype(jnp.float32)
        top_s, top_i = lax.top_k(s, PEER_TOPK)
        cand_s = (top_s[:, :, 0, :, None] + top_s[:, :, 1, None, :]).reshape(blk, PEER_HEADS, PEER_TOPK * PEER_TOPK)
        cand_i = (top_i[:, :, 0, :, None] * N_KEYS + top_i[:, :, 1, None, :]).reshape(blk, PEER_HEADS, PEER_TOPK * PEER_TOPK)
        sel_s, sel_pos = lax.top_k(cand_s, PEER_TOPK)
        idx = jnp.take_along_axis(cand_i, sel_pos, axis=-1)
        gate = jax.nn.softmax(sel_s, axis=-1)
        u = jnp.take(u_tab, idx, axis=0)
        act = jax.nn.gelu(jnp.einsum('nhkd,nd->nhk', u, xi).astype(jnp.float32))
        wgt = (gate * act).astype(xi.dtype)
        return jnp.einsum('nhk,nhkd->nd', wgt, jnp.take(v_tab, idx, axis=0))

    return lax.map(one, xb).reshape(Bn, T, D)


def run_group(x, p, s5_re0, s5_im0, hgrn0, cache_k, cache_v, page_table, w):
    h = x
    s5_re, s5_im, hg, ks, vs = [], [], [], [], []
    for layer in range(DEPTH):
        j = layer // 2
        xn = rmsnorm(h, w['norm_mix'][layer])
        if layer % 2 == 0:
            mix, hr, hi, S = even_mixer(xn, j, s5_re0[j], s5_im0[j], hgrn0[j], w)
            s5_re.append(hr)
            s5_im.append(hi)
            hg.append(S)
        else:
            mix, kr, vr = odd_mixer(xn, j, layer, cache_k, cache_v, page_table, w)
            ks.append(kr)
            vs.append(vr)
        h = h + mix
        h = h + peer(rmsnorm(h, w['norm_ffn'][layer]), w['peer_wq'][layer], w['peer_keys'][layer],
                     w['peer_u'][layer], w['peer_v'][layer])
        gate = jax.nn.sigmoid(rmsnorm(h, w['norm_ple'][layer]) @ w['ple_gate'][layer])
        h = h + (p[layer] @ w['ple_proj'][layer]) * gate
    return h, jnp.stack(s5_re), jnp.stack(s5_im), jnp.stack(hg), jnp.stack(ks), jnp.stack(vs)


def setup_inputs(seed: int = 0) -> dict:
    key = jax.random.key(seed)
    keys = iter(jax.random.split(key, 64))
    f32 = jnp.float32

    def nrm(shape, scale):
        return jax.random.normal(next(keys), shape, f32) * scale

    def gain(shape):
        return 1.0 + nrm(shape, 0.02)

    n_pages = PAST_LEN // PAGE_SIZE
    n_used = DEC_BATCH * n_pages
    n_pool = n_used + n_used // 4
    page_table = jax.random.permutation(next(keys), n_pool)[:n_used].reshape(DEC_BATCH, n_pages).astype(jnp.int32)
    n_idx = jnp.arange(A_STATE, dtype=f32)
    log_dt = jax.random.uniform(next(keys), (N_EVEN, A_GROUPS), f32, math.log(1e-3), math.log(1e-1))
    return {
        'x_prompt': nrm((BATCH, SEQ, D_MODEL), 1.0),
        'x_sample': nrm((DEC_BATCH, DEC_SEQ, D_MODEL), 1.0),
        'p_prompt': nrm((DEPTH, BATCH, SEQ, PLE_DIM), 1.0),
        'p_sample': nrm((DEPTH, DEC_BATCH, DEC_SEQ, PLE_DIM), 1.0),
        'state_s5_re': nrm((N_EVEN, DEC_BATCH, A_GROUPS, A_STATE), 0.5),
        'state_s5_im': nrm((N_EVEN, DEC_BATCH, A_GROUPS, A_STATE), 0.5),
        'state_hgrn': nrm((N_EVEN, DEC_BATCH, B_HEADS, B_DK, B_DK), 0.5),
        'cache_k': nrm((N_ODD, n_pool, PAGE_SIZE, C_HEADS, 2 * C_HEAD_DIM), 1.0),
        'cache_v': nrm((N_ODD, n_pool, PAGE_SIZE, C_HEADS, C_VDIM), 1.0),
        'page_table': page_table,
        'norm_mix': gain((DEPTH, D_MODEL)),
        'norm_ffn': gain((DEPTH, D_MODEL)),
        'norm_ple': gain((DEPTH, D_MODEL)),
        'w_in_ab': nrm((N_EVEN, D_MODEL, AB_COLS), D_MODEL ** -0.5),
        'w_out_ab': nrm((N_EVEN, D_MIX_AB, D_MODEL), D_MIX_AB ** -0.5),
        's5_a_re': -0.5 + nrm((N_EVEN, A_GROUPS, A_STATE), 0.01),
        's5_a_im': math.pi * n_idx + nrm((N_EVEN, A_GROUPS, A_STATE), 0.01),
        's5_log_dt': log_dt,
        's5_b_re': nrm((N_EVEN, A_GROUPS, A_STATE, A_GROUP), (2 * A_GROUP) ** -0.5),
        's5_b_im': nrm((N_EVEN, A_GROUPS, A_STATE, A_GROUP), (2 * A_GROUP) ** -0.5),
        's5_c_re': nrm((N_EVEN, A_GROUPS, A_GROUP, A_STATE), A_STATE ** -0.5),
        's5_c_im': nrm((N_EVEN, A_GROUPS, A_GROUP, A_STATE), A_STATE ** -0.5),
        's5_d': nrm((N_EVEN, A_WIDTH), 0.5),
        's5_glu_w': nrm((N_EVEN, A_WIDTH, A_WIDTH), A_WIDTH ** -0.5),
        's5_glu_b': nrm((N_EVEN, A_WIDTH), 0.01),
        'hgrn_lower_bounds': nrm((N_EVEN + 1, B_WIDTH), 0.1),
        'hgrn_norm': gain((N_EVEN, B_DK)),
        'w_in_c': nrm((N_ODD, D_MODEL, C_COLS), D_MODEL ** -0.5),
        'w_out_c': nrm((N_ODD, C_HEADS * C_VDIM, D_MODEL), (C_HEADS * C_VDIM) ** -0.5),
        'c_q_norm': gain((N_ODD, 2, C_HEAD_DIM)),
        'c_k_norm': gain((N_ODD, 2, C_HEAD_DIM)),
        'c_lam_q1': nrm((N_ODD, C_HEAD_DIM), 0.1),
        'c_lam_k1': nrm((N_ODD, C_HEAD_DIM), 0.1),
        'c_lam_q2': nrm((N_ODD, C_HEAD_DIM), 0.1),
        'c_lam_k2': nrm((N_ODD, C_HEAD_DIM), 0.1),
        'c_subln': gain((N_ODD, C_VDIM)),
        'rel_bias': nrm((REL_BUCKETS, C_HEADS), 0.5),
        'peer_wq': nrm((DEPTH, D_MODEL, PEER_HEADS * PEER_QDIM), D_MODEL ** -0.5),
        'peer_keys': nrm((DEPTH, PEER_HEADS, 2, N_KEYS, PEER_QDIM // 2), (PEER_QDIM // 2) ** -0.5),
        'peer_u': nrm((DEPTH, N_EXPERTS, D_MODEL), D_MODEL ** -0.5),
        'peer_v': nrm((DEPTH, N_EXPERTS, D_MODEL), 0.3),
        'ple_proj': nrm((DEPTH, PLE_DIM, D_MODEL), PLE_DIM ** -0.5),
        'ple_gate': nrm((DEPTH, D_MODEL, D_MODEL), D_MODEL ** -0.5),
    }


def reference(x_prompt, x_sample, p_prompt, p_sample, state_s5_re, state_s5_im, state_hgrn, cache_k, cache_v,
              page_table, norm_mix, norm_ffn, norm_ple, w_in_ab, w_out_ab, s5_a_re, s5_a_im, s5_log_dt,
              s5_b_re, s5_b_im, s5_c_re, s5_c_im, s5_d, s5_glu_w, s5_glu_b, hgrn_lower_bounds, hgrn_norm,
              w_in_c, w_out_c, c_q_norm, c_k_norm, c_lam_q1, c_lam_k1, c_lam_q2, c_lam_k2, c_subln, rel_bias,
              peer_wq, peer_keys, peer_u, peer_v, ple_proj, ple_gate):
    w = dict(norm_mix=norm_mix, norm_ffn=norm_ffn, norm_ple=norm_ple, w_in_ab=w_in_ab, w_out_ab=w_out_ab,
             s5_a_re=s5_a_re, s5_a_im=s5_a_im, s5_log_dt=s5_log_dt, s5_b_re=s5_b_re, s5_b_im=s5_b_im,
             s5_c_re=s5_c_re, s5_c_im=s5_c_im, s5_d=s5_d, s5_glu_w=s5_glu_w, s5_glu_b=s5_glu_b,
             hgrn_lower_bounds=hgrn_lower_bounds, hgrn_norm=hgrn_norm, w_in_c=w_in_c, w_out_c=w_out_c,
             c_q_norm=c_q_norm, c_k_norm=c_k_norm, c_lam_q1=c_lam_q1, c_lam_k1=c_lam_k1, c_lam_q2=c_lam_q2,
             c_lam_k2=c_lam_k2, c_subln=c_subln, rel_bias=rel_bias, peer_wq=peer_wq, peer_keys=peer_keys,
             peer_u=peer_u, peer_v=peer_v, ple_proj=ple_proj, ple_gate=ple_gate)
    Bp = x_prompt.shape[0]
    z_s5 = jnp.zeros((N_EVEN, Bp, A_GROUPS, A_STATE), x_prompt.dtype)
    z_hg = jnp.zeros((N_EVEN, Bp, B_HEADS, B_DK, B_DK), x_prompt.dtype)
    y_prompt, s5r_p, s5i_p, hg_p, k_p, v_p = run_group(x_prompt, p_prompt, z_s5, z_s5, z_hg, None, None, None, w)
    y_sample, s5r_s, s5i_s, hg_s, k_s, v_s = run_group(x_sample, p_sample, state_s5_re, state_s5_im, state_hgrn,
                                                       cache_k, cache_v, page_table, w)
    return (y_prompt, y_sample, s5r_p, s5i_p, hg_p, k_p, v_p, s5r_s, s5i_s, hg_s, k_s, v_s)
```

```python
import functools
import math

import jax
import jax.numpy as jnp
from jax import lax
from jax.experimental import pallas as pl
from jax.experimental.pallas import tpu as pltpu

F32 = jnp.float32
BF16 = jnp.bfloat16
I32 = jnp.int32

D_MODEL = 1024
EPS = 1e-6
A_WIDTH = 512
A_GROUP = 16
A_GROUPS = 32
A_STATE = 64
S5_CHUNK = 8
S5_BLOCKS = 4
S5_GPB = A_GROUPS // S5_BLOCKS
S5_LANES = S5_CHUNK * 128
B_WIDTH = 512
B_DK = 128
B_HEADS = 4
B_CHUNK_MAX = 16
HGRN_ROWS = 512
HGRN_SPAN = 128
C_HEADS = 8
C_HEAD_DIM = 64
C_VDIM = 128
C_WIDTH = C_HEADS * C_VDIM
ATT_BLOCK = 256
REL_BUCKETS = 32
REL_EXACT = REL_BUCKETS // 2
REL_MAX_DIST = 128
PAGE_SIZE = 128
NEW_KEYS_PAD = 128
PEER_HEADS = 8
N_KEYS = 128
PEER_TOPK = 16
PEER_QDIM = 256
PEER_A_PER_STEP = 2
PEER_CAND = 80
NEG_BIG = -1e30

V7X_VMEM_LIMIT = 60000 * 1024


def _params(sem):
    return pltpu.CompilerParams(dimension_semantics=sem, vmem_limit_bytes=V7X_VMEM_LIMIT)


def _rms(x, g):
    ms = jnp.mean(x * x, axis=-1, keepdims=True)
    return x * lax.rsqrt(ms + EPS) * g


def _gelu_tanh(x):
    return 0.5 * x * (1.0 + jnp.tanh(math.sqrt(2.0 / math.pi) * (x + 0.044715 * (x * x * x))))


def _sigmoid(x):
    return 1.0 / (1.0 + jnp.exp(-x))


def _split3(x):
    hi = x.astype(BF16)
    r1 = x - hi.astype(F32)
    mid = r1.astype(BF16)
    lo = (r1 - mid.astype(F32)).astype(BF16)
    return hi, mid, lo


def _dot(a, b):
    return jnp.dot(a, b, preferred_element_type=F32)


def _dot_nt(a, b):
    return lax.dot_general(a, b, (((1,), (1,)), ((), ())), preferred_element_type=F32)


def _dot_exact_rhs(sel, x):
    hi, mid, lo = _split3(x)
    return _dot(sel, hi) + _dot(sel, mid) + _dot(sel, lo)


def _dot_exact_lhs(x, sel):
    hi, mid, lo = _split3(x)
    return _dot(hi, sel) + _dot(mid, sel) + _dot(lo, sel)


def _row_tile(n, pref):
    t = min(pref, n)
    while n % t:
        t //= 2
    return t


def _log2(n):
    assert n & (n - 1) == 0
    return n.bit_length() - 1


def _even_in_kernel(h_ref, g_ref, w_ref, u4_ref, rest_ref):
    xn = _rms(h_ref[...], g_ref[...]).astype(BF16)
    y = _dot(xn, w_ref[...])
    for j in range(S5_BLOCKS):
        u4_ref[j] = y[:, j * 128:(j + 1) * 128]
    rest_ref[...] = y[:, A_WIDTH:]


def even_in_proj(h, g, w_bf16):
    n = h.shape[0]
    tm = _row_tile(n, 256)
    cols = w_bf16.shape[1]
    return pl.pallas_call(
        _even_in_kernel,
        grid=(n // tm,),
        in_specs=[pl.BlockSpec((tm, D_MODEL), lambda i: (i, 0)),
                  pl.BlockSpec((1, D_MODEL), lambda i: (0, 0)),
                  pl.BlockSpec((D_MODEL, cols), lambda i: (0, 0))],
        out_specs=[pl.BlockSpec((S5_BLOCKS, tm, 128), lambda i: (0, i, 0)),
                   pl.BlockSpec((tm, cols - A_WIDTH), lambda i: (i, 0))],
        out_shape=[jax.ShapeDtypeStruct((S5_BLOCKS, n, 128), F32),
                   jax.ShapeDtypeStruct((n, cols - A_WIDTH), F32)],
        compiler_params=_params(("parallel",)),
        name="even_in_proj",
    )(h, g, w_bf16)


def _even_out_kernel(y4_ref, u4_ref, d_ref, gw_ref, gb_ref, yb_ref, wo_ref, h_ref, o_ref):
    y = jnp.concatenate([y4_ref[j] for j in range(S5_BLOCKS)], axis=-1)
    u = jnp.concatenate([u4_ref[j] for j in range(S5_BLOCKS)], axis=-1)
    y = _gelu_tanh(y + d_ref[...] * u)
    ya = y * _sigmoid(_dot(y.astype(BF16), gw_ref[...]) + gb_ref[...])
    mix = _dot(ya.astype(BF16), wo_ref[:A_WIDTH, :]) + _dot(yb_ref[...].astype(BF16), wo_ref[A_WIDTH:, :])
    o_ref[...] = h_ref[...] + mix


def even_out_proj(y4, u4, d, glu_w, glu_b, yb, w_out, h):
    n = h.shape[0]
    tm = _row_tile(n, 512)
    row = lambda i: (i, 0)
    fixed = lambda i: (0, 0)
    return pl.pallas_call(
        _even_out_kernel,
        grid=(n // tm,),
        in_specs=[pl.BlockSpec((S5_BLOCKS, tm, 128), lambda i: (0, i, 0)),
                  pl.BlockSpec((S5_BLOCKS, tm, 128), lambda i: (0, i, 0)),
                  pl.BlockSpec((1, A_WIDTH), fixed),
                  pl.BlockSpec((A_WIDTH, A_WIDTH), fixed),
                  pl.BlockSpec((1, A_WIDTH), fixed),
                  pl.BlockSpec((tm, B_WIDTH), row),
                  pl.BlockSpec((A_WIDTH + B_WIDTH, D_MODEL), fixed),
                  pl.BlockSpec((tm, D_MODEL), row)],
        out_specs=pl.BlockSpec((tm, D_MODEL), row),
        out_shape=jax.ShapeDtypeStruct((n, D_MODEL), F32),
        compiler_params=_params(("parallel",)),
        name="even_out_proj",
    )(y4, u4, d, glu_w, glu_b, yb, w_out, h)


def _res_proj_kernel(x_ref, w_ref, h_ref, o_ref):
    o_ref[...] = h_ref[...] + _dot(x_ref[...].astype(BF16), w_ref[...])


def residual_proj(x, w_bf16, h):
    n = h.shape[0]
    tm = _row_tile(n, 512)
    return pl.pallas_call(
        _res_proj_kernel,
        grid=(n // tm,),
        in_specs=[pl.BlockSpec((tm, x.shape[1]), lambda i: (i, 0)),
                  pl.BlockSpec(w_bf16.shape, lambda i: (0, 0)),
                  pl.BlockSpec((tm, D_MODEL), lambda i: (i, 0))],
        out_specs=pl.BlockSpec((tm, D_MODEL), lambda i: (i, 0)),
        out_shape=jax.ShapeDtypeStruct((n, D_MODEL), F32),
        compiler_params=_params(("parallel",)),
        name="residual_proj",
    )(x, w_bf16, h)


def _ple_kernel(h_ref, pt_ref, p_ref, g_ref, wg_ref, wp_ref, o_ref):
    h2 = h_ref[...] + pt_ref[...].T
    gate = _sigmoid(_dot(_rms(h2, g_ref[...]).astype(BF16), wg_ref[...]))
    o_ref[...] = h2 + _dot(p_ref[...].astype(BF16), wp_ref[...]) * gate


def ple_step(h, peer_t, p, g, w_gate, w_proj):
    n = h.shape[0]
    tm = _row_tile(n, 512)
    row = lambda i: (i, 0)
    fixed = lambda i: (0, 0)
    return pl.pallas_call(
        _ple_kernel,
        grid=(n // tm,),
        in_specs=[pl.BlockSpec((tm, D_MODEL), row),
                  pl.BlockSpec((D_MODEL, tm), lambda i: (0, i)),
                  pl.BlockSpec((tm, p.shape[1]), row),
                  pl.BlockSpec((1, D_MODEL), fixed),
                  pl.BlockSpec((D_MODEL, D_MODEL), fixed),
                  pl.BlockSpec((p.shape[1], D_MODEL), fixed)],
        out_specs=pl.BlockSpec((tm, D_MODEL), row),
        out_shape=jax.ShapeDtypeStruct((n, D_MODEL), F32),
        compiler_params=_params(("parallel",)),
        name="ple_step",
    )(h, peer_t, p, g, w_gate, w_proj)


def s5_tables(a_re, a_im, log_dt, b_re, b_im, c_re, c_im):
    hp = lax.Precision.HIGHEST
    L = S5_CHUNK
    dt = jnp.exp(log_dt.astype(F32))[:, None]
    a_re = a_re.astype(F32)
    a_im = a_im.astype(F32)
    mag = jnp.exp(a_re * dt)
    lb_re = mag * jnp.cos(a_im * dt)
    lb_im = mag * jnp.sin(a_im * dt)
    den = a_re * a_re + a_im * a_im
    coef_re = ((lb_re - 1.0) * a_re + lb_im * a_im) / den
    coef_im = (lb_im * a_re - (lb_re - 1.0) * a_im) / den
    bb_re = coef_re[..., None] * b_re - coef_im[..., None] * b_im
    bb_im = coef_re[..., None] * b_im + coef_im[..., None] * b_re
    tau = jnp.arange(L + 1, dtype=F32)[:, None, None]
    pmag = jnp.exp(tau * (a_re * dt))
    pw_re = pmag * jnp.cos(tau * (a_im * dt))
    pw_im = pmag * jnp.sin(tau * (a_im * dt))
    lb_b_re = pw_re[..., None] * bb_re - pw_im[..., None] * bb_im
    lb_b_im = pw_re[..., None] * bb_im + pw_im[..., None] * bb_re
    c_re = c_re.astype(F32)
    c_im = c_im.astype(F32)
    ktau = (jnp.einsum('gcp,lgpd->lgcd', c_re, lb_b_re, precision=hp)
            - jnp.einsum('gcp,lgpd->lgcd', c_im, lb_b_im, precision=hp))
    s_idx = jnp.arange(L)[:, None]
    t_idx = jnp.arange(L)[None, :]
    lag = t_idx - s_idx
    kl = jnp.where((lag >= 0)[:, :, None, None, None], ktau[jnp.clip(lag, 0, L)], 0.0)
    kl = kl.reshape(L, L, S5_BLOCKS, S5_GPB, A_GROUP, A_GROUP)
    eye = jnp.eye(S5_GPB, dtype=F32)
    m = jnp.einsum('stjgcd,gh->jshdtgc', kl, eye).reshape(S5_BLOCKS, S5_LANES, S5_LANES)
    rev = jnp.arange(L - 1, -1, -1)
    pr = lb_b_re[rev].reshape(L, S5_BLOCKS, S5_GPB, A_STATE, A_GROUP)
    pi = lb_b_im[rev].reshape(L, S5_BLOCKS, S5_GPB, A_STATE, A_GROUP)
    half = S5_GPB * A_STATE
    p_re = jnp.einsum('sjgpd,gh->jshdgp', pr, eye).reshape(S5_BLOCKS, S5_LANES, half)
    p_im = jnp.einsum('sjgpd,gh->jshdgp', pi, eye).reshape(S5_BLOCKS, S5_LANES, half)
    p = jnp.concatenate([p_re, p_im], axis=-1)
    cl_re = c_re[None] * pw_re[1:, :, None, :] - c_im[None] * pw_im[1:, :, None, :]
    cl_im = c_re[None] * pw_im[1:, :, None, :] + c_im[None] * pw_re[1:, :, None, :]
    cl_re = cl_re.reshape(L, S5_BLOCKS, S5_GPB, A_GROUP, A_STATE)
    cl_im = cl_im.reshape(L, S5_BLOCKS, S5_GPB, A_GROUP, A_STATE)
    q_re = jnp.einsum('tjgcp,gh->jhptgc', cl_re, eye).reshape(S5_BLOCKS, half, S5_LANES)
    q_im = jnp.einsum('tjgcp,gh->jhptgc', -cl_im, eye).reshape(S5_BLOCKS, half, S5_LANES)
    q = jnp.concatenate([q_re, q_im], axis=-2)
    lam_l = jnp.concatenate([pw_re[L].reshape(S5_BLOCKS, 1, half), pw_im[L].reshape(S5_BLOCKS, 1, half)], axis=-1)
    return m.astype(BF16), p.astype(BF16), q.astype(BF16), lam_l


def _s5_chunk_kernel(u_ref, m_ref, p_ref, ya_ref, xt_ref):
    u = u_ref[0].astype(BF16)
    ya_ref[0] = _dot(u, m_ref[0])
    xt_ref[0] = _dot(u, p_ref[0])


def s5_chunk_matmul(u8, m, p):
    r = u8.shape[1]
    tr = _row_tile(r, 512)
    blk = lambda j, i: (j, i, 0)
    wblk = lambda j, i: (j, 0, 0)
    return pl.pallas_call(
        _s5_chunk_kernel,
        grid=(S5_BLOCKS, r // tr),
        in_specs=[pl.BlockSpec((1, tr, S5_LANES), blk),
                  pl.BlockSpec((1, S5_LANES, S5_LANES), wblk),
                  pl.BlockSpec((1, S5_LANES, S5_LANES), wblk)],
        out_specs=[pl.BlockSpec((1, tr, S5_LANES), blk), pl.BlockSpec((1, tr, S5_LANES), blk)],
        out_shape=[jax.ShapeDtypeStruct(u8.shape, F32), jax.ShapeDtypeStruct(u8.shape, F32)],
        compiler_params=_params(("parallel", "parallel")),
        name="s5_chunk_matmul",
    )(u8, m, p)


def _cplx_step(lam, h, x, half):
    lr, li = lam[:, :half], lam[:, half:]
    hr, hi = h[:, :half], h[:, half:]
    return jnp.concatenate([lr * hr - li * hi + x[:, :half], lr * hi + li * hr + x[:, half:]], axis=-1)


def _s5_scan_kernel(xt_ref, ya_ref, h0_ref, lam_ref, q_ref, y_ref, hfin_ref, hprev_ref):
    half = S5_LANES // 2
    lam = lam_ref[0]
    nchunks = xt_ref.shape[1]

    def body(k, h):
        hprev_ref[pl.ds(k, 1), :] = h
        return _cplx_step(lam, h, xt_ref[0, pl.ds(k, 1), :], half)

    hfin_ref[0, 0] = lax.fori_loop(0, nchunks, body, h0_ref[0, 0])
    y_ref[0] = ya_ref[0] + _dot(hprev_ref[...].astype(BF16), q_ref[0])


def _s5_step_kernel(xt_ref, ya_ref, h0_ref, lam_ref, q_ref, y_ref, hfin_ref):
    h0 = h0_ref[0]
    hfin_ref[0] = _cplx_step(lam_ref[0], h0, xt_ref[0], S5_LANES // 2)
    y_ref[0] = ya_ref[0] + _dot(h0.astype(BF16), q_ref[0])


def s5_scan(xt, ya, h0, lam_l, q, batch):
    r = xt.shape[1]
    nchunks = r // batch
    wblk = lambda j, b: (j, 0, 0)
    seq = lambda j, b: (j, b, 0)
    if nchunks > 1:
        st = lambda j, b: (j, b, 0, 0)
        y, hfin = pl.pallas_call(
            _s5_scan_kernel,
            grid=(S5_BLOCKS, batch),
            in_specs=[pl.BlockSpec((1, nchunks, S5_LANES), seq),
                      pl.BlockSpec((1, nchunks, S5_LANES), seq),
                      pl.BlockSpec((1, 1, 1, S5_LANES), st),
                      pl.BlockSpec((1, 1, S5_LANES), wblk),
                      pl.BlockSpec((1, S5_LANES, S5_LANES), wblk)],
            out_specs=[pl.BlockSpec((1, nchunks, S5_LANES), seq),
                       pl.BlockSpec((1, 1, 1, S5_LANES), st)],
            out_shape=[jax.ShapeDtypeStruct(ya.shape, F32),
                       jax.ShapeDtypeStruct((S5_BLOCKS, batch, 1, S5_LANES), F32)],
            scratch_shapes=[pltpu.VMEM((nchunks, S5_LANES), F32)],
            compiler_params=_params(("parallel", "parallel")),
            name="s5_scan",
        )(xt, ya, h0.reshape(S5_BLOCKS, batch, 1, S5_LANES), lam_l, q)
        return y, hfin.reshape(S5_BLOCKS, batch, S5_LANES)
    tb = _row_tile(batch, 128)
    return pl.pallas_call(
        _s5_step_kernel,
        grid=(S5_BLOCKS, batch // tb),
        in_specs=[pl.BlockSpec((1, tb, S5_LANES), seq),
                  pl.BlockSpec((1, tb, S5_LANES), seq),
                  pl.BlockSpec((1, tb, S5_LANES), seq),
                  pl.BlockSpec((1, 1, S5_LANES), wblk),
                  pl.BlockSpec((1, S5_LANES, S5_LANES), wblk)],
        out_specs=[pl.BlockSpec((1, tb, S5_LANES), seq), pl.BlockSpec((1, tb, S5_LANES), seq)],
        out_shape=[jax.ShapeDtypeStruct(ya.shape, F32),
                   jax.ShapeDtypeStruct((S5_BLOCKS, batch, S5_LANES), F32)],
        compiler_params=_params(("parallel", "parallel")),
        name="s5_step",
    )(xt, ya, h0, lam_l, q)


def _s5_state_to_blocks(s_re, s_im):
    b = s_re.shape[0]
    half = S5_GPB * A_STATE
    re = s_re.reshape(b, S5_BLOCKS, half).transpose(1, 0, 2)
    im = s_im.reshape(b, S5_BLOCKS, half).transpose(1, 0, 2)
    return jnp.concatenate([re, im], axis=-1)


def _s5_blocks_to_state(hb):
    b = hb.shape[1]
    half = S5_GPB * A_STATE
    re = hb[:, :, :half].transpose(1, 0, 2).reshape(b, A_GROUPS, A_STATE)
    im = hb[:, :, half:].transpose(1, 0, 2).reshape(b, A_GROUPS, A_STATE)
    return re, im


def s5_mixer(u4, tabs, s_re, s_im, batch):
    m, p, q, lam_l = tabs
    n = u4.shape[1]
    u8 = u4.reshape(S5_BLOCKS, n // S5_CHUNK, S5_LANES)
    ya, xt = s5_chunk_matmul(u8, m, p)
    y8, hfin = s5_scan(xt, ya, _s5_state_to_blocks(s_re, s_im), lam_l, q, batch)
    re, im = _s5_blocks_to_state(hfin)
    return y8.reshape(S5_BLOCKS, n, 128), re, im


def _hgrn_kernel(x_ref, lb_ref, ng_ref, s0_ref, y_ref, st_out_ref,
                 o_scr, qe_scr, ke_scr, dec_scr, vt_scr, st_scr, *, chunk, carry):
    rows = x_ref.shape[0]
    nchunks = rows // chunk
    span = min(rows, HGRN_SPAN)
    W = B_WIDTH
    heads = [slice(h * B_DK, (h + 1) * B_DK) for h in range(B_HEADS)]
    q = x_ref[:, 0:W]
    fpre = x_ref[:, W:2 * W]
    v = x_ref[:, 2 * W:3 * W]
    lb = lb_ref[...]
    f = lb + (1.0 - lb) * _sigmoid(fpre)
    lf = jnp.log(f)
    k = 1.0 - f
    sh = _log2(chunk)
    ti = lax.broadcasted_iota(I32, (rows, rows), 0)
    si = lax.broadcasted_iota(I32, (rows, rows), 1)
    same = lax.shift_right_logical(ti, sh) == lax.shift_right_logical(si, sh)
    tri = jnp.where(same & (si <= ti), 1.0, 0.0).astype(BF16)
    ones = jnp.where(same, 1.0, 0.0).astype(BF16)
    b = _dot_exact_rhs(tri, lf)
    bl = _dot_exact_rhs(ones, lf)
    pos = lax.broadcasted_iota(I32, (rows, W), 0) & (chunk - 1)
    o = jnp.zeros((rows, W), F32)
    for j in range(chunk):
        if j == 0:
            ks, bs, vs = k, b, v
        else:
            ks, bs, vs = pltpu.roll(k, j, 0), pltpu.roll(b, j, 0), pltpu.roll(v, j, 0)
        ok = pos >= j
        w = q * ks * jnp.exp(jnp.where(ok, b - bs, 0.0))
        contrib = jnp.concatenate(
            [jnp.sum(w[:, sl], axis=-1, keepdims=True) * vs[:, sl] for sl in heads], axis=-1)
        o = o + jnp.where(ok, contrib, 0.0)
    o_scr[...] = o
    qe_scr[...] = q * jnp.exp(b)
    ke_scr[...] = k * jnp.exp(bl - b)
    dec_scr[...] = jnp.exp(bl)
    for t in range(rows // span):
        vt_scr[t] = v[t * span:(t + 1) * span, :].T

    if carry:
        @pl.when(pl.program_id(1) == 0)
        def _():
            st_scr[...] = s0_ref[0]

    def chunk_body(c, carry_):
        r0 = pl.multiple_of(c * chunk, chunk)
        t = lax.shift_right_logical(c * chunk, _log2(span))
        base = pl.multiple_of(t * span, span)
        rloc = lax.broadcasted_iota(I32, (span, B_DK), 0) + base
        inchunk = (rloc >= r0) & (rloc < r0 + chunk)
        for h, sl in enumerate(heads):
            st = st_scr[h] if carry else s0_ref[c, h]
            qe_c = qe_scr[pl.ds(r0, chunk), sl].astype(BF16)
            o_scr[pl.ds(r0, chunk), sl] += _dot_nt(qe_c, st.astype(BF16))
            ke_blk = jnp.where(inchunk, ke_scr[pl.ds(base, span), sl], 0.0).astype(BF16)
            new = st * dec_scr[pl.ds(r0, 1), sl] + _dot(vt_scr[t, sl, :].astype(BF16), ke_blk)
            if carry:
                st_scr[h] = new
            else:
                st_out_ref[c, h] = new
        return carry_

    lax.fori_loop(0, nchunks, chunk_body, 0)

    if carry:
        @pl.when(pl.program_id(1) == pl.num_programs(1) - 1)
        def _():
            st_out_ref[0] = st_scr[...]

    o = o_scr[...]
    g = x_ref[:, 3 * W:4 * W]
    ng = ng_ref[...]
    outs = []
    for sl in heads:
        oh = o[:, sl]
        ms = jnp.mean(oh * oh, axis=-1, keepdims=True)
        outs.append(oh * lax.rsqrt(ms + EPS) * ng[:, sl])
    y_ref[...] = jnp.concatenate(outs, axis=-1) * (g * _sigmoid(g))


def hgrn_mixer(x, lb, ng, s0t, batch, seqlen):
    n = x.shape[0]
    chunk = math.gcd(seqlen, B_CHUNK_MAX)
    scratch = lambda rows: [pltpu.VMEM((rows, B_WIDTH), F32)] * 4 + [
        pltpu.VMEM((max(rows // HGRN_SPAN, 1), B_WIDTH, min(rows, HGRN_SPAN)), F32),
        pltpu.VMEM((B_HEADS, B_DK, B_DK), F32)]
    st_shape = jax.ShapeDtypeStruct((batch, B_HEADS, B_DK, B_DK), F32)
    fixed2 = lambda *_: (0, 0)
    if seqlen > chunk:
        rows = _row_tile(seqlen, HGRN_ROWS)
        steps = seqlen // rows
        return pl.pallas_call(
            functools.partial(_hgrn_kernel, chunk=chunk, carry=True),
            grid=(batch, steps),
            in_specs=[pl.BlockSpec((rows, 4 * B_WIDTH), lambda b, i: (b * steps + i, 0)),
                      pl.BlockSpec((1, B_WIDTH), fixed2),
                      pl.BlockSpec((1, B_WIDTH), fixed2),
                      pl.BlockSpec((1, B_HEADS, B_DK, B_DK), lambda b, i: (b, 0, 0, 0))],
            out_specs=[pl.BlockSpec((rows, B_WIDTH), lambda b, i: (b * steps + i, 0)),
                       pl.BlockSpec((1, B_HEADS, B_DK, B_DK), lambda b, i: (b, 0, 0, 0))],
            out_shape=[jax.ShapeDtypeStruct((n, B_WIDTH), F32), st_shape],
            scratch_shapes=scratch(rows),
            compiler_params=_params(("parallel", "arbitrary")),
            name="hgrn_carry",
        )(x, lb, ng, s0t)
    seqs = _row_tile(batch, HGRN_SPAN // chunk)
    rows = seqs * chunk
    return pl.pallas_call(
        functools.partial(_hgrn_kernel, chunk=chunk, carry=False),
        grid=(batch // seqs,),
        in_specs=[pl.BlockSpec((rows, 4 * B_WIDTH), lambda i: (i, 0)),
                  pl.BlockSpec((1, B_WIDTH), fixed2),
                  pl.BlockSpec((1, B_WIDTH), fixed2),
                  pl.BlockSpec((seqs, B_HEADS, B_DK, B_DK), lambda i: (i, 0, 0, 0))],
        out_specs=[pl.BlockSpec((rows, B_WIDTH), lambda i: (i, 0)),
                   pl.BlockSpec((seqs, B_HEADS, B_DK, B_DK), lambda i: (i, 0, 0, 0))],
        out_shape=[jax.ShapeDtypeStruct((n, B_WIDTH), F32), st_shape],
        scratch_shapes=scratch(rows),
        compiler_params=_params(("parallel",)),
        name="hgrn_single",
    )(x, lb, ng, s0t)


def _segnorm(x, gain, seg, segt):
    ms = _dot_exact_lhs(x * x, seg) * (1.0 / C_HEAD_DIM)
    inv = _dot_exact_lhs(lax.rsqrt(ms + EPS), segt)
    return x * inv * gain


def _odd_in_kernel(h_ref, g_ref, w_ref, gq_ref, gk_ref, seg_ref, segt_ref,
                   qs_ref, kf_ref, kb_ref, vf_ref, vb_ref):
    xn = _rms(h_ref[...], g_ref[...]).astype(BF16)
    y = _dot(xn, w_ref[...])
    seg = seg_ref[...]
    segt = segt_ref[...]
    q = _segnorm(y[:, :C_WIDTH], gq_ref[...], seg, segt)
    k = _segnorm(y[:, C_WIDTH:2 * C_WIDTH], gk_ref[...], seg, segt)
    v = y[:, 2 * C_WIDTH:]
    qs_ref[...] = (q * (C_HEAD_DIM ** -0.5)).astype(BF16)
    kf_ref[...] = k
    kb_ref[...] = k.astype(BF16)
    vf_ref[...] = v
    vb_ref[...] = v.astype(BF16)


def odd_in_proj(h, g, w_bf16, gq, gk):
    n = h.shape[0]
    tm = _row_tile(n, 256)
    lane = jnp.arange(C_WIDTH) // C_HEAD_DIM
    seg = (lane[:, None] == jnp.arange(128)[None, :]).astype(BF16)
    row = lambda i: (i, 0)
    fixed = lambda i: (0, 0)
    wide = pl.BlockSpec((tm, C_WIDTH), row)
    return pl.pallas_call(
        _odd_in_kernel,
        grid=(n // tm,),
        in_specs=[pl.BlockSpec((tm, D_MODEL), row),
                  pl.BlockSpec((1, D_MODEL), fixed),
                  pl.BlockSpec(w_bf16.shape, fixed),
                  pl.BlockSpec((1, C_WIDTH), fixed),
                  pl.BlockSpec((1, C_WIDTH), fixed),
                  pl.BlockSpec((C_WIDTH, 128), fixed),
                  pl.BlockSpec((128, C_WIDTH), fixed)],
        out_specs=[wide, wide, wide, wide, wide],
        out_shape=[jax.ShapeDtypeStruct((n, C_WIDTH), BF16), jax.ShapeDtypeStruct((n, C_WIDTH), F32),
                   jax.ShapeDtypeStruct((n, C_WIDTH), BF16), jax.ShapeDtypeStruct((n, C_WIDTH), F32),
                   jax.ShapeDtypeStruct((n, C_WIDTH), BF16)],
        compiler_params=_params(("parallel",)),
        name="odd_in_proj",
    )(h, g, w_bf16, gq, gk, seg, seg.T)


def _rel_bias(dist, tab_ref, h):
    nf = jnp.maximum(dist, 1).astype(F32)
    large = REL_EXACT + (jnp.log(nf / REL_EXACT) / math.log(REL_MAX_DIST / REL_EXACT)
                         * (REL_BUCKETS - REL_EXACT)).astype(I32)
    bucket = jnp.where(dist < REL_EXACT, dist, jnp.minimum(large, REL_BUCKETS - 1))
    far = tab_ref[REL_BUCKETS - 1, h]
    out = jnp.zeros(dist.shape, F32)
    for bkt in range(REL_BUCKETS - 1):
        out = jnp.where(bucket == bkt, tab_ref[bkt, h] - far, out)
    return out


def _bias_tiles_kernel(tab_ref, o_ref):
    h = pl.program_id(0)
    blk = o_ref.shape[-1]
    r = lax.broadcasted_iota(I32, (blk, blk), 0)
    c = lax.broadcasted_iota(I32, (blk, blk), 1)
    for delta in range(2):
        dist = delta * blk + r - c
        o_ref[0, delta] = jnp.where(dist >= 0, _rel_bias(jnp.maximum(dist, 0), tab_ref, h), NEG_BIG)


def bias_tiles(rel_bias, blk):
    return pl.pallas_call(
        _bias_tiles_kernel,
        grid=(C_HEADS,),
        in_specs=[pl.BlockSpec(memory_space=pltpu.SMEM)],
        out_specs=pl.BlockSpec((1, 2, blk, blk), lambda h: (h, 0, 0, 0)),
        out_shape=jax.ShapeDtypeStruct((C_HEADS, 2, blk, blk), F32),
        compiler_params=_params(("parallel",)),
        name="bias_tiles",
    )(rel_bias)


def _bias_sample_kernel(tab_ref, o_ref, *, t_new):
    rows = o_ref.shape[0]
    r = lax.broadcasted_iota(I32, (rows, PAGE_SIZE + NEW_KEYS_PAD), 0)
    c = lax.broadcasted_iota(I32, (rows, PAGE_SIZE + NEW_KEYS_PAD), 1)
    tok = lax.shift_right_logical(r, _log2(2 * C_HEADS))
    head = lax.shift_right_logical(r, 1) & (C_HEADS - 1)
    dist = jnp.where(c < PAGE_SIZE, PAGE_SIZE + tok - c, tok - (c - PAGE_SIZE))
    valid = (dist >= 0) & (c < PAGE_SIZE + t_new)
    out = jnp.zeros(dist.shape, F32)
    for h in range(C_HEADS):
        out = jnp.where(head == h, _rel_bias(jnp.maximum(dist, 0), tab_ref, h), out)
    o_ref[...] = jnp.where(valid, out, NEG_BIG)


def bias_sample(rel_bias, t_new):
    rows = t_new * C_HEADS * 2
    return pl.pallas_call(
        functools.partial(_bias_sample_kernel, t_new=t_new),
        in_specs=[pl.BlockSpec(memory_space=pltpu.SMEM)],
        out_specs=pl.BlockSpec(memory_space=pltpu.VMEM),
        out_shape=jax.ShapeDtypeStruct((rows, PAGE_SIZE + NEW_KEYS_PAD), F32),
        name="bias_sample",
    )(rel_bias)


def _softmax_update(s, v, m_ref, l_ref, acc_ref, idx):
    m_old = m_ref[idx]
    m_new = jnp.maximum(m_old, jnp.max(s, axis=-1, keepdims=True))
    alpha = jnp.exp(m_old - m_new)
    p = jnp.exp(s - m_new)
    l_ref[idx] = alpha * l_ref[idx] + jnp.sum(p, axis=-1, keepdims=True)
    acc_ref[idx] = alpha * acc_ref[idx] + _dot(p.astype(BF16), v)
    m_ref[idx] = m_new


def _subln(o, gain, lam_init):
    ms = jnp.mean(o * o, axis=-1, keepdims=True)
    return o * lax.rsqrt(ms + EPS) * gain * (1.0 - lam_init)


def _attn_prompt_kernel(lam_ref, q_ref, k_ref, v_ref, bias_ref, gs_ref, o_ref, m_scr, l_scr, acc_scr, *, lam_init):
    qi = pl.program_id(2)
    blk = q_ref.shape[1]
    q = q_ref[0]
    lane = lax.broadcasted_iota(I32, q.shape, 1)
    zero = jnp.zeros_like(q)
    qmaps = (jnp.where(lane < C_HEAD_DIM, q, zero), jnp.where(lane >= C_HEAD_DIM, q, zero))
    m_scr[...] = jnp.full(m_scr.shape, NEG_BIG, F32)
    l_scr[...] = jnp.zeros(l_scr.shape, F32)
    acc_scr[...] = jnp.zeros(acc_scr.shape, F32)

    def step(kb, bias):
        start = pl.multiple_of(kb * blk, blk)
        k = k_ref[0, pl.ds(start, blk), :]
        v = v_ref[0, pl.ds(start, blk), :]
        for c in range(2):
            s = _dot_nt(qmaps[c], k)
            if bias is not None:
                s = s + bias
            _softmax_update(s, v, m_scr, l_scr, acc_scr, c)

    def far_body(kb, carry):
        step(kb, None)
        return carry

    lax.fori_loop(0, jnp.maximum(qi - 1, 0), far_body, 0)

    @pl.when(qi >= 1)
    def _():
        step(qi - 1, bias_ref[0, 1])

    step(qi, bias_ref[0, 0])
    o = acc_scr[0] / l_scr[0] - lam_ref[0] * (acc_scr[1] / l_scr[1])
    o_ref[0] = _subln(o, gs_ref[...], lam_init)


def attn_prompt(qs, kb, vb, bias, lam, gsub, lam_init):
    b, t, _ = qs.shape
    blk = bias.shape[-1]
    return pl.pallas_call(
        functools.partial(_attn_prompt_kernel, lam_init=lam_init),
        grid=(b, C_HEADS, t // blk),
        in_specs=[pl.BlockSpec(memory_space=pltpu.SMEM),
                  pl.BlockSpec((1, blk, C_VDIM), lambda bi, h, i: (bi, i, h)),
                  pl.BlockSpec((1, t, C_VDIM), lambda bi, h, i: (bi, 0, h)),
                  pl.BlockSpec((1, t, C_VDIM), lambda bi, h, i: (bi, 0, h)),
                  pl.BlockSpec((1, 2, blk, blk), lambda bi, h, i: (h, 0, 0, 0)),
                  pl.BlockSpec((1, C_VDIM), lambda bi, h, i: (0, 0))],
        out_specs=pl.BlockSpec((1, blk, C_VDIM), lambda bi, h, i: (bi, i, h)),
        out_shape=jax.ShapeDtypeStruct((b, t, C_WIDTH), F32),
        scratch_shapes=[pltpu.VMEM((2, blk, 1), F32), pltpu.VMEM((2, blk, 1), F32),
                        pltpu.VMEM((2, blk, C_VDIM), F32)],
        compiler_params=_params(("parallel", "parallel", "parallel")),
        name="attn_prompt",
    )(lam, qs, kb, vb, bias, gsub)


def _attn_sample_kernel(pt_ref, lam_ref, q_ref, kc_ref, vc_ref, kn_ref, vn_ref, bias_ref, gs_ref, o_ref,
                        qe_scr, m_scr, l_scr, acc_scr, *, lam_init):
    del pt_ref
    page = pl.program_id(1)
    last = pl.num_programs(1) - 1
    t_new = q_ref.shape[1]
    rows = qe_scr.shape[0]
    per_tok = 2 * C_HEADS

    @pl.when(page == 0)
    def _():
        qf = q_ref[0].astype(F32)
        seg = lax.shift_right_logical(lax.broadcasted_iota(I32, (per_tok, C_WIDTH), 1), _log2(C_HEAD_DIM))
        own = seg == lax.broadcasted_iota(I32, (per_tok, C_WIDTH), 0)
        qe = [jnp.where(own, jnp.broadcast_to(qf[t:t + 1], (per_tok, C_WIDTH)), 0.0) for t in range(t_new)]
        qe_scr[...] = jnp.concatenate(qe, axis=0).astype(BF16)
        m_scr[...] = jnp.full(m_scr.shape, NEG_BIG, F32)
        l_scr[...] = jnp.zeros(l_scr.shape, F32)
        acc_scr[...] = jnp.zeros(acc_scr.shape, F32)

    qe = qe_scr[...]
    s = _dot_nt(qe, kc_ref[0].astype(BF16))
    s = s + jnp.where(page == last, bias_ref[:, :PAGE_SIZE], 0.0)
    _softmax_update(s, vc_ref[0].astype(BF16), m_scr, l_scr, acc_scr, 0)

    @pl.when(page == last)
    def _():
        pad = jnp.zeros((NEW_KEYS_PAD - t_new, C_WIDTH), F32)
        kn = jnp.concatenate([kn_ref[0], pad], axis=0).astype(BF16)
        vn = jnp.concatenate([vn_ref[0], pad], axis=0).astype(BF16)
        s2 = _dot_nt(qe, kn) + bias_ref[:, PAGE_SIZE:]
        _softmax_update(s2, vn, m_scr, l_scr, acc_scr, 0)
        r = lax.broadcasted_iota(I32, (rows, 1), 0)
        coef = jnp.where((r & 1) == 0, 1.0, -lam_ref[0])
        head = lax.shift_right_logical(r, 1) & (C_HEADS - 1)
        lane_head = lax.shift_right_logical(lax.broadcasted_iota(I32, (rows, C_WIDTH), 1), _log2(C_VDIM))
        x = jnp.where(lane_head == head, acc_scr[0] * (coef / l_scr[0]), 0.0)
        o = jnp.sum(x.reshape(t_new, per_tok, C_WIDTH), axis=1)
        gs = gs_ref[...]
        outs = [_subln(o[:, h * C_VDIM:(h + 1) * C_VDIM], gs, lam_init) for h in range(C_HEADS)]
        o_ref[0] = jnp.concatenate(outs, axis=-1)


def attn_sample(qs, kn, vn, cache_k, cache_v, page_table, bias, lam, gsub, lam_init):
    b, t_new, _ = qs.shape
    n_pages = page_table.shape[1]
    rows = t_new * 2 * C_HEADS
    seq = lambda bi, p, pt: (bi, 0, 0)
    paged = lambda bi, p, pt: (pt[bi * n_pages + p], 0, 0)
    fixed = lambda bi, p, pt: (0, 0)
    grid_spec = pltpu.PrefetchScalarGridSpec(
        num_scalar_prefetch=1,
        grid=(b, n_pages),
        in_specs=[pl.BlockSpec(memory_space=pltpu.SMEM),
                  pl.BlockSpec((1, t_new, C_WIDTH), seq),
                  pl.BlockSpec((1, PAGE_SIZE, C_WIDTH), paged),
                  pl.BlockSpec((1, PAGE_SIZE, C_WIDTH), paged),
                  pl.BlockSpec((1, t_new, C_WIDTH), seq),
                  pl.BlockSpec((1, t_new, C_WIDTH), seq),
                  pl.BlockSpec((rows, PAGE_SIZE + NEW_KEYS_PAD), fixed),
                  pl.BlockSpec((1, C_VDIM), fixed)],
        out_specs=pl.BlockSpec((1, t_new, C_WIDTH), seq),
        scratch_shapes=[pltpu.VMEM((rows, C_WIDTH), BF16),
                        pltpu.VMEM((1, rows, 1), F32), pltpu.VMEM((1, rows, 1), F32),
                        pltpu.VMEM((1, rows, C_WIDTH), F32)])
    return pl.pallas_call(
        functools.partial(_attn_sample_kernel, lam_init=lam_init),
        grid_spec=grid_spec,
        out_shape=jax.ShapeDtypeStruct((b, t_new, C_WIDTH), F32),
        compiler_params=_params(("parallel", "arbitrary")),
        name="attn_sample",
    )(page_table.reshape(-1), lam, qs, cache_k, cache_v, kn, vn, bias, gsub)


def _col_max(x):
    return jnp.max(x, axis=0, keepdims=True)


def _col_min(x):
    return jnp.min(x, axis=0, keepdims=True)


def _extract_top(work, order, k):
    r, tn = work.shape
    krow = lax.broadcasted_iota(I32, (k, tn), 0)

    def body(i, carry):
        work, rank, tops = carry
        m = _col_max(work)
        pick = _col_min(jnp.where(work == m, order, float(r * r)))
        chosen = order == pick
        rank = jnp.where(chosen, i.astype(F32), rank)
        work = jnp.where(chosen, -jnp.inf, work)
        tops = jnp.where(krow == i, m, tops)
        return work, rank, tops

    init = (work, jnp.full((r, tn), float(k), F32), jnp.zeros((k, tn), F32))
    _, rank, tops = lax.fori_loop(0, k, body, init)
    return rank, tops


def _peer_prep_kernel(h_ref, g_ref, wq_ref, keys_ref, xn_ref, r2_ref, e2_ref, cnt_ref, c_ref):
    K = PEER_TOPK
    xn = _rms(h_ref[...], g_ref[...]).astype(BF16)
    xn_ref[...] = xn
    tn = xn.shape[0]
    row_keys = lax.broadcasted_iota(I32, (N_KEYS, 1), 0).astype(F32)
    ci = lax.broadcasted_iota(I32, (PEER_CAND, 1), 0)
    mid = ci - K
    cand_p = jnp.where(ci < K, 0, jnp.where(ci < PEER_CAND - 8, 1 + lax.shift_right_logical(mid, 3), ci - (PEER_CAND - 16)))
    cand_q = jnp.where(ci < K, ci, jnp.where(ci < PEER_CAND - 8, mid & 7, 0))
    cand_pos = (cand_p * K + cand_q).astype(F32)
    prow = lax.broadcasted_iota(I32, (8, 1), 0)

    def pair_table(a, b, op):
        blocks = [op(a[0:1], b)] + [op(a[p:p + 1], b[0:8]) for p in range(1, 8)] + [op(a[8:16], b[0:1])]
        return jnp.concatenate(blocks, axis=0)

    def head_body(h, carry):
        w = wq_ref[pl.ds(pl.multiple_of(h * PEER_QDIM, PEER_QDIM), PEER_QDIM), :]
        qt = _dot_nt(w, xn).astype(BF16)
        s1 = _dot(keys_ref[h, 0], qt[:N_KEYS])
        s2 = _dot(keys_ref[h, 1], qt[N_KEYS:])
        r1, top1 = _extract_top(s1, row_keys, K)
        r2, top2 = _extract_top(s2, row_keys, K)
        sel_rank, _ = _extract_top(pair_table(top1, top2, jnp.add), cand_pos, K)
        sel = sel_rank < K
        e1 = jnp.exp(top1 - top1[0:1])
        e2 = jnp.exp(top2 - top2[0:1])
        z = jnp.sum(jnp.where(sel, pair_table(e1, e2, jnp.multiply), 0.0), axis=0, keepdims=True)
        self32 = sel.astype(F32)
        cnt_lo = jnp.zeros((8, tn), F32)
        cnt_lo = jnp.where(prow == 0, jnp.sum(self32[0:K], axis=0, keepdims=True), cnt_lo)
        for p in range(1, 8):
            blk = self32[K + 8 * (p - 1):K + 8 * p]
            cnt_lo = jnp.where(prow == p, jnp.sum(blk, axis=0, keepdims=True), cnt_lo)
        cnt16 = jnp.concatenate([cnt_lo, self32[PEER_CAND - 8:]], axis=0)
        cnt_a = jnp.zeros((N_KEYS, tn), F32)
        for p in range(K):
            cnt_a = jnp.where(r1 == float(p), cnt16[p:p + 1], cnt_a)
        r2_ref[h] = r2
        e2_ref[h] = jnp.exp(s2 - top2[0:1])
        cnt_ref[h] = cnt_a
        c_ref[h] = jnp.exp(s1 - top1[0:1]) / z
        return carry

    lax.fori_loop(0, PEER_HEADS, head_body, 0)


def peer_prep(h, g, wq_t, keys):
    n = h.shape[0]
    tn = _row_tile(n, 256)
    tab = jax.ShapeDtypeStruct((PEER_HEADS, N_KEYS, n), F32)
    tspec = pl.BlockSpec((PEER_HEADS, N_KEYS, tn), lambda i: (0, 0, i))
    return pl.pallas_call(
        _peer_prep_kernel,
        grid=(n // tn,),
        in_specs=[pl.BlockSpec((tn, D_MODEL), lambda i: (i, 0)),
                  pl.BlockSpec((1, D_MODEL), lambda i: (0, 0)),
                  pl.BlockSpec(wq_t.shape, lambda i: (0, 0)),
                  pl.BlockSpec(keys.shape, lambda i: (0, 0, 0, 0))],
        out_specs=[pl.BlockSpec((tn, D_MODEL), lambda i: (i, 0)), tspec, tspec, tspec, tspec],
        out_shape=[jax.ShapeDtypeStruct((n, D_MODEL), BF16), tab, tab, tab, tab],
        compiler_params=_params(("parallel",)),
        name="peer_prep",
    )(h, g, wq_t, keys)


def _peer_dense_kernel(xn_ref, u_ref, vt_ref, r2_ref, e2_ref, cnt_ref, c_ref, o_ref):
    j = pl.program_id(1)

    @pl.when(j == 0)
    def _():
        o_ref[...] = jnp.zeros(o_ref.shape, F32)

    act = _gelu_tanh(_dot_nt(u_ref[...], xn_ref[...]))
    tn = act.shape[1]
    gates = []
    for i in range(PEER_A_PER_STEP):
        a = j * PEER_A_PER_STEP + i
        g = jnp.zeros((N_KEYS, tn), F32)
        for h in range(PEER_HEADS):
            sel = r2_ref[h] < cnt_ref[h, pl.ds(a, 1), :]
            g = g + jnp.where(sel, e2_ref[h], 0.0) * c_ref[h, pl.ds(a, 1), :]
        gates.append(g)
    w = (jnp.concatenate(gates, axis=0) * act).astype(BF16)
    o_ref[...] += _dot(vt_ref[...], w)


def peer_dense(xn, u_bf16, vt_bf16, r2, e2, cnt, c):
    n = xn.shape[0]
    tn = _row_tile(n, 512)
    ne = PEER_A_PER_STEP * N_KEYS
    tspec = pl.BlockSpec((PEER_HEADS, N_KEYS, tn), lambda i, j: (0, 0, i))
    return pl.pallas_call(
        _peer_dense_kernel,
        grid=(n // tn, N_KEYS // PEER_A_PER_STEP),
        in_specs=[pl.BlockSpec((tn, D_MODEL), lambda i, j: (i, 0)),
                  pl.BlockSpec((ne, D_MODEL), lambda i, j: (j, 0)),
                  pl.BlockSpec((D_MODEL, ne), lambda i, j: (0, j)),
                  tspec, tspec, tspec, tspec],
        out_specs=pl.BlockSpec((D_MODEL, tn), lambda i, j: (0, i)),
        out_shape=jax.ShapeDtypeStruct((D_MODEL, n), F32),
        compiler_params=_params(("parallel", "arbitrary")),
        name="peer_dense",
    )(xn, u_bf16, vt_bf16, r2, e2, cnt, c)


def _row(x):
    return x.astype(F32).reshape(1, -1)


def _run_group(x, p, s5_re0, s5_im0, hgrn0, cache_k, cache_v, page_table, w):
    batch, seqlen, _ = x.shape
    n = batch * seqlen
    depth = p.shape[0]
    h = x.reshape(n, D_MODEL)
    s5_re, s5_im, hg, ks, vs = [], [], [], [], []
    for layer in range(depth):
        j = layer // 2
        if layer % 2 == 0:
            u4, rest = even_in_proj(h, _row(w['norm_mix'][layer]), w['w_in_ab'][j])
            y4, hr, hi = s5_mixer(u4, w['s5_tabs'][j], s5_re0[j], s5_im0[j], batch)
            yb, st = hgrn_mixer(rest, w['hgrn_lb'][j], w['hgrn_ng'][j],
                                jnp.swapaxes(hgrn0[j], -1, -2), batch, seqlen)
            h = even_out_proj(y4, u4, _row(w['s5_d'][j]), w['s5_glu_w'][j], _row(w['s5_glu_b'][j]),
                              yb, w['w_out_ab'][j], h)
            s5_re.append(hr)
            s5_im.append(hi)
            hg.append(jnp.swapaxes(st, -1, -2))
        else:
            qs, kf, kb, vf, vb = odd_in_proj(h, _row(w['norm_mix'][layer]), w['w_in_c'][j],
                                             w['c_gq'][j], w['c_gk'][j])
            lam_init = 0.8 - 0.6 * math.exp(-0.3 * layer)
            lam = w['c_lam'][j] + lam_init
            gsub = _row(w['c_subln'][j])
            shape3 = (batch, seqlen, C_WIDTH)
            if page_table is None:
                o = attn_prompt(qs.reshape(shape3), kb.reshape(shape3), vb.reshape(shape3),
                                w['bias_tiles'], lam, gsub, lam_init)
            else:
                ck = cache_k[j].reshape(-1, PAGE_SIZE, C_WIDTH)
                cv = cache_v[j].reshape(-1, PAGE_SIZE, C_WIDTH)
                o = attn_sample(qs.reshape(shape3), kf.reshape(shape3), vf.reshape(shape3), ck, cv,
                                page_table, w['bias_sample'], lam, gsub, lam_init)
            h = residual_proj(o.reshape(n, C_WIDTH), w['w_out_c'][j], h)
            ks.append(kf.reshape(batch, seqlen, C_HEADS, 2 * C_HEAD_DIM))
            vs.append(vf.reshape(batch, seqlen, C_HEADS, C_VDIM))
        xn, r2, e2, cnt, c = peer_prep(h, _row(w['norm_ffn'][layer]), w['peer_wq_t'][layer], w['peer_keys'][layer])
        peer_t = peer_dense(xn, w['peer_u'][layer], w['peer_v_t'][layer], r2, e2, cnt, c)
        h = ple_step(h, peer_t, p[layer].reshape(n, -1), _row(w['norm_ple'][layer]),
                     w['ple_gate'][layer], w['ple_proj'][layer])
    return (h.reshape(batch, seqlen, D_MODEL), jnp.stack(s5_re), jnp.stack(s5_im), jnp.stack(hg),
            jnp.stack(ks), jnp.stack(vs))


def kernel(x_prompt, x_sample, p_prompt, p_sample, state_s5_re, state_s5_im, state_hgrn, cache_k, cache_v,
           page_table, norm_mix, norm_ffn, norm_ple, w_in_ab, w_out_ab, s5_a_re, s5_a_im, s5_log_dt,
           s5_b_re, s5_b_im, s5_c_re, s5_c_im, s5_d, s5_glu_w, s5_glu_b, hgrn_lower_bounds, hgrn_norm,
           w_in_c, w_out_c, c_q_norm, c_k_norm, c_lam_q1, c_lam_k1, c_lam_q2, c_lam_k2, c_subln, rel_bias,
           peer_wq, peer_keys, peer_u, peer_v, ple_proj, ple_gate):
    n_even = w_in_ab.shape[0]
    n_odd = w_in_c.shape[0]
    t_new = x_sample.shape[1]
    lb_all = jnp.cumsum(jax.nn.softmax(hgrn_lower_bounds.astype(F32), axis=0), axis=0)
    tile_heads = lambda g, reps: jnp.tile(g.astype(F32).reshape(-1), reps).reshape(1, -1)
    lam_dot = lambda a, b: jnp.exp(jnp.sum(a.astype(F32) * b.astype(F32), axis=-1))
    w = dict(
        norm_mix=norm_mix, norm_ffn=norm_ffn, norm_ple=norm_ple,
        w_in_ab=w_in_ab.astype(BF16), w_out_ab=w_out_ab.astype(BF16),
        s5_tabs=[s5_tables(s5_a_re[j], s5_a_im[j], s5_log_dt[j], s5_b_re[j], s5_b_im[j], s5_c_re[j], s5_c_im[j])
                 for j in range(n_even)],
        s5_d=s5_d, s5_glu_w=s5_glu_w.astype(BF16), s5_glu_b=s5_glu_b,
        hgrn_lb=[lb_all[j].reshape(1, -1) for j in range(n_even)],
        hgrn_ng=[tile_heads(hgrn_norm[j], B_HEADS) for j in range(n_even)],
        w_in_c=w_in_c.astype(BF16), w_out_c=w_out_c.astype(BF16),
        c_gq=[tile_heads(c_q_norm[j], C_HEADS) for j in range(n_odd)],
        c_gk=[tile_heads(c_k_norm[j], C_HEADS) for j in range(n_odd)],
        c_lam=[(lam_dot(c_lam_q1[j], c_lam_k1[j]) - lam_dot(c_lam_q2[j], c_lam_k2[j])).reshape(1)
               for j in range(n_odd)],
        c_subln=c_subln,
        bias_tiles=bias_tiles(rel_bias.astype(F32), ATT_BLOCK),
        bias_sample=bias_sample(rel_bias.astype(F32), t_new),
        peer_wq_t=jnp.swapaxes(peer_wq, -1, -2).astype(BF16),
        peer_keys=peer_keys.astype(BF16),
        peer_u=peer_u.astype(BF16),
        peer_v_t=jnp.swapaxes(peer_v, -1, -2).astype(BF16),
        ple_proj=ple_proj.astype(BF16), ple_gate=ple_gate.astype(BF16),
    )
    bp = x_prompt.shape[0]
    z_s5 = jnp.zeros((n_even, bp, A_GROUPS, A_STATE), F32)
    z_hg = jnp.zeros((n_even, bp, B_HEADS, B_DK, B_DK), F32)
    y_p, s5r_p, s5i_p, hg_p, k_p, v_p = _run_group(x_prompt, p_prompt, z_s5, z_s5, z_hg, None, None, None, w)
    y_s, s5r_s, s5i_s, hg_s, k_s, v_s = _run_group(x_sample, p_sample, state_s5_re, state_s5_im, state_hgrn,
                                                   cache_k, cache_v, page_table, w)
    return (y_p, y_s, s5r_p, s5i_p, hg_p, k_p, v_p, s5r_s, s5i_s, hg_s, k_s, v_s)
```

```python
import functools
import math

import jax
import jax.numpy as jnp
from jax import lax
from jax.experimental import pallas as pl
from jax.experimental.pallas import tpu as pltpu

F32 = jnp.float32
BF16 = jnp.bfloat16
I32 = jnp.int32

D_MODEL = 1024
EPS = 1e-6
A_WIDTH = 512
A_GROUP = 16
A_GROUPS = 32
A_STATE = 64
S5_CHUNK = 8
S5_BLOCKS = 4
S5_GPB = A_GROUPS // S5_BLOCKS
S5_LANES = S5_CHUNK * 128
B_WIDTH = 512
B_DK = 128
B_HEADS = 4
B_CHUNK_MAX = 16
HGRN_ROWS = 512
HGRN_SPAN = 128
C_HEADS = 8
C_HEAD_DIM = 64
C_VDIM = 128
C_WIDTH = C_HEADS * C_VDIM
ATT_BLOCK = 512
REL_BUCKETS = 32
REL_EXACT = REL_BUCKETS // 2
REL_MAX_DIST = 128
PAGE_SIZE = 128
NEW_KEYS_PAD = 128
SAMPLE_PAGES_PER_STEP = 4
PEER_HEADS = 8
N_KEYS = 128
PEER_TOPK = 16
PEER_QDIM = 256
PEER_A_PER_STEP = 8
PEER_A_INNER = 1
PEER_B_ROWS = 128
PEER_LANES = 256
PEER_CAND = 80
NEG_BIG = -1e30

V7X_VMEM_LIMIT = 60000 * 1024


def _params(sem, flags=None):
    return pltpu.CompilerParams(dimension_semantics=sem, vmem_limit_bytes=V7X_VMEM_LIMIT, flags=flags)


def _rms(x, g):
    ms = jnp.mean(x * x, axis=-1, keepdims=True)
    return x * lax.rsqrt(ms + EPS) * g


def _gelu_tanh(x):
    return 0.5 * x * (1.0 + jnp.tanh(math.sqrt(2.0 / math.pi) * (x + 0.044715 * (x * x * x))))


def _sigmoid(x):
    return 1.0 / (1.0 + jnp.exp(-x))


def _split3(x):
    hi = x.astype(BF16)
    r1 = x - hi.astype(F32)
    mid = r1.astype(BF16)
    lo = (r1 - mid.astype(F32)).astype(BF16)
    return hi, mid, lo


def _dot(a, b):
    return jnp.dot(a, b, preferred_element_type=F32)


def _dot_nt(a, b):
    return lax.dot_general(a, b, (((1,), (1,)), ((), ())), preferred_element_type=F32)


def _dot_exact_rhs(sel, x):
    hi, mid, lo = _split3(x)
    return _dot(sel, hi) + _dot(sel, mid) + _dot(sel, lo)


def _dot_exact_lhs(x, sel):
    hi, mid, lo = _split3(x)
    return _dot(hi, sel) + _dot(mid, sel) + _dot(lo, sel)


def _row_tile(n, pref):
    t = min(pref, n)
    while n % t:
        t //= 2
    return t


def _log2(n):
    assert n & (n - 1) == 0
    return n.bit_length() - 1


def _even_in_kernel(h_ref, g_ref, w_ref, u4_ref, rest_ref):
    xn = _rms(h_ref[...], g_ref[...]).astype(BF16)
    y = _dot(xn, w_ref[...])
    for j in range(S5_BLOCKS):
        u4_ref[j] = y[:, j * 128:(j + 1) * 128]
    rest_ref[...] = y[:, A_WIDTH:]


def even_in_proj(h, g, w_bf16):
    n = h.shape[0]
    tm = _row_tile(n, 256)
    cols = w_bf16.shape[1]
    return pl.pallas_call(
        _even_in_kernel,
        grid=(n // tm,),
        in_specs=[pl.BlockSpec((tm, D_MODEL), lambda i: (i, 0)),
                  pl.BlockSpec((1, D_MODEL), lambda i: (0, 0)),
                  pl.BlockSpec((D_MODEL, cols), lambda i: (0, 0))],
        out_specs=[pl.BlockSpec((S5_BLOCKS, tm, 128), lambda i: (0, i, 0)),
                   pl.BlockSpec((tm, cols - A_WIDTH), lambda i: (i, 0))],
        out_shape=[jax.ShapeDtypeStruct((S5_BLOCKS, n, 128), F32),
                   jax.ShapeDtypeStruct((n, cols - A_WIDTH), F32)],
        compiler_params=_params(("parallel",)),
        name="even_in_proj",
    )(h, g, w_bf16)


def _even_out_kernel(y4_ref, u4_ref, d_ref, gw_ref, gb_ref, yb_ref, wo_ref, h_ref, o_ref):
    y = jnp.concatenate([y4_ref[j] for j in range(S5_BLOCKS)], axis=-1)
    u = jnp.concatenate([u4_ref[j] for j in range(S5_BLOCKS)], axis=-1)
    y = _gelu_tanh(y + d_ref[...] * u)
    ya = y * _sigmoid(_dot(y.astype(BF16), gw_ref[...]) + gb_ref[...])
    mix = _dot(ya.astype(BF16), wo_ref[:A_WIDTH, :]) + _dot(yb_ref[...].astype(BF16), wo_ref[A_WIDTH:, :])
    o_ref[...] = h_ref[...] + mix


def even_out_proj(y4, u4, d, glu_w, glu_b, yb, w_out, h):
    n = h.shape[0]
    tm = _row_tile(n, 512)
    row = lambda i: (i, 0)
    fixed = lambda i: (0, 0)
    return pl.pallas_call(
        _even_out_kernel,
        grid=(n // tm,),
        in_specs=[pl.BlockSpec((S5_BLOCKS, tm, 128), lambda i: (0, i, 0)),
                  pl.BlockSpec((S5_BLOCKS, tm, 128), lambda i: (0, i, 0)),
                  pl.BlockSpec((1, A_WIDTH), fixed),
                  pl.BlockSpec((A_WIDTH, A_WIDTH), fixed),
                  pl.BlockSpec((1, A_WIDTH), fixed),
                  pl.BlockSpec((tm, B_WIDTH), row),
                  pl.BlockSpec((A_WIDTH + B_WIDTH, D_MODEL), fixed),
                  pl.BlockSpec((tm, D_MODEL), row)],
        out_specs=pl.BlockSpec((tm, D_MODEL), row),
        out_shape=jax.ShapeDtypeStruct((n, D_MODEL), F32),
        compiler_params=_params(("parallel",)),
        name="even_out_proj",
    )(y4, u4, d, glu_w, glu_b, yb, w_out, h)


def _res_proj_kernel(x_ref, w_ref, h_ref, o_ref):
    o_ref[...] = h_ref[...] + _dot(x_ref[...].astype(BF16), w_ref[...])


def residual_proj(x, w_bf16, h):
    n = h.shape[0]
    tm = _row_tile(n, 512)
    return pl.pallas_call(
        _res_proj_kernel,
        grid=(n // tm,),
        in_specs=[pl.BlockSpec((tm, x.shape[1]), lambda i: (i, 0)),
                  pl.BlockSpec(w_bf16.shape, lambda i: (0, 0)),
                  pl.BlockSpec((tm, D_MODEL), lambda i: (i, 0))],
        out_specs=pl.BlockSpec((tm, D_MODEL), lambda i: (i, 0)),
        out_shape=jax.ShapeDtypeStruct((n, D_MODEL), F32),
        compiler_params=_params(("parallel",)),
        name="residual_proj",
    )(x, w_bf16, h)


def _ple_kernel(h_ref, pt_ref, p_ref, g_ref, wg_ref, wp_ref, o_ref):
    h2 = h_ref[...] + pt_ref[...].T
    gate = _sigmoid(_dot(_rms(h2, g_ref[...]).astype(BF16), wg_ref[...]))
    o_ref[...] = h2 + _dot(p_ref[...].astype(BF16), wp_ref[...]) * gate


def ple_step(h, peer_t, p, g, w_gate, w_proj):
    n = h.shape[0]
    tm = _row_tile(n, 512)
    row = lambda i: (i, 0)
    fixed = lambda i: (0, 0)
    return pl.pallas_call(
        _ple_kernel,
        grid=(n // tm,),
        in_specs=[pl.BlockSpec((tm, D_MODEL), row),
                  pl.BlockSpec((D_MODEL, tm), lambda i: (0, i)),
                  pl.BlockSpec((tm, p.shape[1]), row),
                  pl.BlockSpec((1, D_MODEL), fixed),
                  pl.BlockSpec((D_MODEL, D_MODEL), fixed),
                  pl.BlockSpec((p.shape[1], D_MODEL), fixed)],
        out_specs=pl.BlockSpec((tm, D_MODEL), row),
        out_shape=jax.ShapeDtypeStruct((n, D_MODEL), F32),
        compiler_params=_params(("parallel",)),
        name="ple_step",
    )(h, peer_t, p, g, w_gate, w_proj)


def s5_tables(a_re, a_im, log_dt, b_re, b_im, c_re, c_im):
    hp = lax.Precision.HIGHEST
    L = S5_CHUNK
    dt = jnp.exp(log_dt.astype(F32))[:, None]
    a_re = a_re.astype(F32)
    a_im = a_im.astype(F32)
    mag = jnp.exp(a_re * dt)
    lb_re = mag * jnp.cos(a_im * dt)
    lb_im = mag * jnp.sin(a_im * dt)
    den = a_re * a_re + a_im * a_im
    coef_re = ((lb_re - 1.0) * a_re + lb_im * a_im) / den
    coef_im = (lb_im * a_re - (lb_re - 1.0) * a_im) / den
    bb_re = coef_re[..., None] * b_re - coef_im[..., None] * b_im
    bb_im = coef_re[..., None] * b_im + coef_im[..., None] * b_re
    tau = jnp.arange(L + 1, dtype=F32)[:, None, None]
    pmag = jnp.exp(tau * (a_re * dt))
    pw_re = pmag * jnp.cos(tau * (a_im * dt))
    pw_im = pmag * jnp.sin(tau * (a_im * dt))
    lb_b_re = pw_re[..., None] * bb_re - pw_im[..., None] * bb_im
    lb_b_im = pw_re[..., None] * bb_im + pw_im[..., None] * bb_re
    c_re = c_re.astype(F32)
    c_im = c_im.astype(F32)
    ktau = (jnp.einsum('gcp,lgpd->lgcd', c_re, lb_b_re, precision=hp)
            - jnp.einsum('gcp,lgpd->lgcd', c_im, lb_b_im, precision=hp))
    s_idx = jnp.arange(L)[:, None]
    t_idx = jnp.arange(L)[None, :]
    lag = t_idx - s_idx
    kl = jnp.where((lag >= 0)[:, :, None, None, None], ktau[jnp.clip(lag, 0, L)], 0.0)
    kl = kl.reshape(L, L, S5_BLOCKS, S5_GPB, A_GROUP, A_GROUP)
    eye = jnp.eye(S5_GPB, dtype=F32)
    m = jnp.einsum('stjgcd,gh->jshdtgc', kl, eye).reshape(S5_BLOCKS, S5_LANES, S5_LANES)
    rev = jnp.arange(L - 1, -1, -1)
    pr = lb_b_re[rev].reshape(L, S5_BLOCKS, S5_GPB, A_STATE, A_GROUP)
    pi = lb_b_im[rev].reshape(L, S5_BLOCKS, S5_GPB, A_STATE, A_GROUP)
    half = S5_GPB * A_STATE
    p_re = jnp.einsum('sjgpd,gh->jshdgp', pr, eye).reshape(S5_BLOCKS, S5_LANES, half)
    p_im = jnp.einsum('sjgpd,gh->jshdgp', pi, eye).reshape(S5_BLOCKS, S5_LANES, half)
    p = jnp.concatenate([p_re, p_im], axis=-1)
    cl_re = c_re[None] * pw_re[1:, :, None, :] - c_im[None] * pw_im[1:, :, None, :]
    cl_im = c_re[None] * pw_im[1:, :, None, :] + c_im[None] * pw_re[1:, :, None, :]
    cl_re = cl_re.reshape(L, S5_BLOCKS, S5_GPB, A_GROUP, A_STATE)
    cl_im = cl_im.reshape(L, S5_BLOCKS, S5_GPB, A_GROUP, A_STATE)
    q_re = jnp.einsum('tjgcp,gh->jhptgc', cl_re, eye).reshape(S5_BLOCKS, half, S5_LANES)
    q_im = jnp.einsum('tjgcp,gh->jhptgc', -cl_im, eye).reshape(S5_BLOCKS, half, S5_LANES)
    q = jnp.concatenate([q_re, q_im], axis=-2)
    lam_l = jnp.concatenate([pw_re[L].reshape(S5_BLOCKS, 1, half), pw_im[L].reshape(S5_BLOCKS, 1, half)], axis=-1)
    return m.astype(BF16), p.astype(BF16), q.astype(BF16), lam_l


def _s5_chunk_kernel(u_ref, m_ref, p_ref, ya_ref, xt_ref):
    u = u_ref[0].astype(BF16)
    ya_ref[0] = _dot(u, m_ref[0])
    xt_ref[0] = _dot(u, p_ref[0])


def s5_chunk_matmul(u8, m, p):
    r = u8.shape[1]
    tr = _row_tile(r, 512)
    blk = lambda j, i: (j, i, 0)
    wblk = lambda j, i: (j, 0, 0)
    return pl.pallas_call(
        _s5_chunk_kernel,
        grid=(S5_BLOCKS, r // tr),
        in_specs=[pl.BlockSpec((1, tr, S5_LANES), blk),
                  pl.BlockSpec((1, S5_LANES, S5_LANES), wblk),
                  pl.BlockSpec((1, S5_LANES, S5_LANES), wblk)],
        out_specs=[pl.BlockSpec((1, tr, S5_LANES), blk), pl.BlockSpec((1, tr, S5_LANES), blk)],
        out_shape=[jax.ShapeDtypeStruct(u8.shape, F32), jax.ShapeDtypeStruct(u8.shape, F32)],
        compiler_params=_params(("parallel", "parallel")),
        name="s5_chunk_matmul",
    )(u8, m, p)


def _cplx_step(lam, h, x, half):
    lr, li = lam[:, :half], lam[:, half:]
    hr, hi = h[:, :half], h[:, half:]
    return jnp.concatenate([lr * hr - li * hi + x[:, :half], lr * hi + li * hr + x[:, half:]], axis=-1)


def _s5_scan_kernel(xt_ref, ya_ref, h0_ref, lam_ref, q_ref, y_ref, hfin_ref, hprev_ref):
    half = S5_LANES // 2
    lam = lam_ref[0]
    nchunks = xt_ref.shape[1]

    def body(k, h):
        hprev_ref[pl.ds(k, 1), :] = h
        return _cplx_step(lam, h, xt_ref[0, pl.ds(k, 1), :], half)

    hfin_ref[0, 0] = lax.fori_loop(0, nchunks, body, h0_ref[0, 0])
    y_ref[0] = ya_ref[0] + _dot(hprev_ref[...].astype(BF16), q_ref[0])


def _s5_step_kernel(xt_ref, ya_ref, h0_ref, lam_ref, q_ref, y_ref, hfin_ref):
    h0 = h0_ref[0]
    hfin_ref[0] = _cplx_step(lam_ref[0], h0, xt_ref[0], S5_LANES // 2)
    y_ref[0] = ya_ref[0] + _dot(h0.astype(BF16), q_ref[0])


def s5_scan(xt, ya, h0, lam_l, q, batch):
    r = xt.shape[1]
    nchunks = r // batch
    wblk = lambda j, b: (j, 0, 0)
    seq = lambda j, b: (j, b, 0)
    if nchunks > 1:
        st = lambda j, b: (j, b, 0, 0)
        y, hfin = pl.pallas_call(
            _s5_scan_kernel,
            grid=(S5_BLOCKS, batch),
            in_specs=[pl.BlockSpec((1, nchunks, S5_LANES), seq),
                      pl.BlockSpec((1, nchunks, S5_LANES), seq),
                      pl.BlockSpec((1, 1, 1, S5_LANES), st),
                      pl.BlockSpec((1, 1, S5_LANES), wblk),
                      pl.BlockSpec((1, S5_LANES, S5_LANES), wblk)],
            out_specs=[pl.BlockSpec((1, nchunks, S5_LANES), seq),
                       pl.BlockSpec((1, 1, 1, S5_LANES), st)],
            out_shape=[jax.ShapeDtypeStruct(ya.shape, F32),
                       jax.ShapeDtypeStruct((S5_BLOCKS, batch, 1, S5_LANES), F32)],
            scratch_shapes=[pltpu.VMEM((nchunks, S5_LANES), F32)],
            compiler_params=_params(("parallel", "parallel")),
            name="s5_scan",
        )(xt, ya, h0.reshape(S5_BLOCKS, batch, 1, S5_LANES), lam_l, q)
        return y, hfin.reshape(S5_BLOCKS, batch, S5_LANES)
    tb = _row_tile(batch, 128)
    return pl.pallas_call(
        _s5_step_kernel,
        grid=(S5_BLOCKS, batch // tb),
        in_specs=[pl.BlockSpec((1, tb, S5_LANES), seq),
                  pl.BlockSpec((1, tb, S5_LANES), seq),
                  pl.BlockSpec((1, tb, S5_LANES), seq),
                  pl.BlockSpec((1, 1, S5_LANES), wblk),
                  pl.BlockSpec((1, S5_LANES, S5_LANES), wblk)],
        out_specs=[pl.BlockSpec((1, tb, S5_LANES), seq), pl.BlockSpec((1, tb, S5_LANES), seq)],
        out_shape=[jax.ShapeDtypeStruct(ya.shape, F32),
                   jax.ShapeDtypeStruct((S5_BLOCKS, batch, S5_LANES), F32)],
        compiler_params=_params(("parallel", "parallel")),
        name="s5_step",
    )(xt, ya, h0, lam_l, q)


def _s5_state_to_blocks(s_re, s_im):
    b = s_re.shape[0]
    half = S5_GPB * A_STATE
    re = s_re.reshape(b, S5_BLOCKS, half).transpose(1, 0, 2)
    im = s_im.reshape(b, S5_BLOCKS, half).transpose(1, 0, 2)
    return jnp.concatenate([re, im], axis=-1)


def _s5_blocks_to_state(hb):
    b = hb.shape[1]
    half = S5_GPB * A_STATE
    re = hb[:, :, :half].transpose(1, 0, 2).reshape(b, A_GROUPS, A_STATE)
    im = hb[:, :, half:].transpose(1, 0, 2).reshape(b, A_GROUPS, A_STATE)
    return re, im


def s5_mixer(u4, tabs, s_re, s_im, batch):
    m, p, q, lam_l = tabs
    n = u4.shape[1]
    u8 = u4.reshape(S5_BLOCKS, n // S5_CHUNK, S5_LANES)
    ya, xt = s5_chunk_matmul(u8, m, p)
    y8, hfin = s5_scan(xt, ya, _s5_state_to_blocks(s_re, s_im), lam_l, q, batch)
    re, im = _s5_blocks_to_state(hfin)
    return y8.reshape(S5_BLOCKS, n, 128), re, im


def _hgrn_kernel(x_ref, lb_ref, ng_ref, s0_ref, y_ref, st_out_ref,
                 o_scr, qe_scr, ke_scr, dec_scr, vt_scr, st_scr, *, chunk, carry):
    rows = x_ref.shape[0]
    nchunks = rows // chunk
    span = min(rows, HGRN_SPAN)
    W = B_WIDTH
    heads = [slice(h * B_DK, (h + 1) * B_DK) for h in range(B_HEADS)]
    q = x_ref[:, 0:W]
    fpre = x_ref[:, W:2 * W]
    v = x_ref[:, 2 * W:3 * W]
    lb = lb_ref[...]
    f = lb + (1.0 - lb) * _sigmoid(fpre)
    lf = jnp.log(f)
    k = 1.0 - f
    sh = _log2(chunk)
    ti = lax.broadcasted_iota(I32, (rows, rows), 0)
    si = lax.broadcasted_iota(I32, (rows, rows), 1)
    same = lax.shift_right_logical(ti, sh) == lax.shift_right_logical(si, sh)
    tri = jnp.where(same & (si <= ti), 1.0, 0.0).astype(BF16)
    ones = jnp.where(same, 1.0, 0.0).astype(BF16)
    b = _dot_exact_rhs(tri, lf)
    bl = _dot_exact_rhs(ones, lf)
    pos = lax.broadcasted_iota(I32, (rows, W), 0) & (chunk - 1)
    o = jnp.zeros((rows, W), F32)
    for j in range(chunk):
        if j == 0:
            ks, bs, vs = k, b, v
        else:
            ks, bs, vs = pltpu.roll(k, j, 0), pltpu.roll(b, j, 0), pltpu.roll(v, j, 0)
        ok = pos >= j
        w = q * ks * jnp.exp(jnp.where(ok, b - bs, 0.0))
        contrib = jnp.concatenate(
            [jnp.sum(w[:, sl], axis=-1, keepdims=True) * vs[:, sl] for sl in heads], axis=-1)
        o = o + jnp.where(ok, contrib, 0.0)
    o_scr[...] = o
    qe_scr[...] = q * jnp.exp(b)
    ke_scr[...] = k * jnp.exp(bl - b)
    dec_scr[...] = jnp.exp(bl)
    for t in range(rows // span):
        vt_scr[t] = v[t * span:(t + 1) * span, :].T

    if carry:
        @pl.when(pl.program_id(1) == 0)
        def _():
            st_scr[...] = s0_ref[0]

    def chunk_body(c, carry_):
        r0 = pl.multiple_of(c * chunk, chunk)
        t = lax.shift_right_logical(c * chunk, _log2(span))
        base = pl.multiple_of(t * span, span)
        rloc = lax.broadcasted_iota(I32, (span, B_DK), 0) + base
        inchunk = (rloc >= r0) & (rloc < r0 + chunk)
        for h, sl in enumerate(heads):
            st = st_scr[h] if carry else s0_ref[c, h]
            qe_c = qe_scr[pl.ds(r0, chunk), sl].astype(BF16)
            o_scr[pl.ds(r0, chunk), sl] += _dot_nt(qe_c, st.astype(BF16))
            ke_blk = jnp.where(inchunk, ke_scr[pl.ds(base, span), sl], 0.0).astype(BF16)
            new = st * dec_scr[pl.ds(r0, 1), sl] + _dot(vt_scr[t, sl, :].astype(BF16), ke_blk)
            if carry:
                st_scr[h] = new
            else:
                st_out_ref[c, h] = new
        return carry_

    lax.fori_loop(0, nchunks, chunk_body, 0)

    if carry:
        @pl.when(pl.program_id(1) == pl.num_programs(1) - 1)
        def _():
            st_out_ref[0] = st_scr[...]

    o = o_scr[...]
    g = x_ref[:, 3 * W:4 * W]
    ng = ng_ref[...]
    outs = []
    for sl in heads:
        oh = o[:, sl]
        ms = jnp.mean(oh * oh, axis=-1, keepdims=True)
        outs.append(oh * lax.rsqrt(ms + EPS) * ng[:, sl])
    y_ref[...] = jnp.concatenate(outs, axis=-1) * (g * _sigmoid(g))


def hgrn_mixer(x, lb, ng, s0t, batch, seqlen):
    n = x.shape[0]
    chunk = math.gcd(seqlen, B_CHUNK_MAX)
    scratch = lambda rows: [pltpu.VMEM((rows, B_WIDTH), F32)] * 4 + [
        pltpu.VMEM((max(rows // HGRN_SPAN, 1), B_WIDTH, min(rows, HGRN_SPAN)), F32),
        pltpu.VMEM((B_HEADS, B_DK, B_DK), F32)]
    st_shape = jax.ShapeDtypeStruct((batch, B_HEADS, B_DK, B_DK), F32)
    fixed2 = lambda *_: (0, 0)
    if seqlen > chunk:
        rows = _row_tile(seqlen, HGRN_ROWS)
        steps = seqlen // rows
        return pl.pallas_call(
            functools.partial(_hgrn_kernel, chunk=chunk, carry=True),
            grid=(batch, steps),
            in_specs=[pl.BlockSpec((rows, 4 * B_WIDTH), lambda b, i: (b * steps + i, 0)),
                      pl.BlockSpec((1, B_WIDTH), fixed2),
                      pl.BlockSpec((1, B_WIDTH), fixed2),
                      pl.BlockSpec((1, B_HEADS, B_DK, B_DK), lambda b, i: (b, 0, 0, 0))],
            out_specs=[pl.BlockSpec((rows, B_WIDTH), lambda b, i: (b * steps + i, 0)),
                       pl.BlockSpec((1, B_HEADS, B_DK, B_DK), lambda b, i: (b, 0, 0, 0))],
            out_shape=[jax.ShapeDtypeStruct((n, B_WIDTH), F32), st_shape],
            scratch_shapes=scratch(rows),
            compiler_params=_params(("parallel", "arbitrary")),
            name="hgrn_carry",
        )(x, lb, ng, s0t)
    seqs = _row_tile(batch, HGRN_SPAN // chunk)
    rows = seqs * chunk
    return pl.pallas_call(
        functools.partial(_hgrn_kernel, chunk=chunk, carry=False),
        grid=(batch // seqs,),
        in_specs=[pl.BlockSpec((rows, 4 * B_WIDTH), lambda i: (i, 0)),
                  pl.BlockSpec((1, B_WIDTH), fixed2),
                  pl.BlockSpec((1, B_WIDTH), fixed2),
                  pl.BlockSpec((seqs, B_HEADS, B_DK, B_DK), lambda i: (i, 0, 0, 0))],
        out_specs=[pl.BlockSpec((rows, B_WIDTH), lambda i: (i, 0)),
                   pl.BlockSpec((seqs, B_HEADS, B_DK, B_DK), lambda i: (i, 0, 0, 0))],
        out_shape=[jax.ShapeDtypeStruct((n, B_WIDTH), F32), st_shape],
        scratch_shapes=scratch(rows),
        compiler_params=_params(("parallel",)),
        name="hgrn_single",
    )(x, lb, ng, s0t)


def _segnorm(x, gain, seg, segt):
    ms = _dot_exact_lhs(x * x, seg) * (1.0 / C_HEAD_DIM)
    inv = _dot_exact_lhs(lax.rsqrt(ms + EPS), segt)
    return x * inv * gain


def _odd_in_kernel(h_ref, g_ref, w_ref, gq_ref, gk_ref, seg_ref, segt_ref,
                   qs_ref, kf_ref, kb_ref, vf_ref, vb_ref):
    xn = _rms(h_ref[...], g_ref[...]).astype(BF16)
    y = _dot(xn, w_ref[...])
    seg = seg_ref[...]
    segt = segt_ref[...]
    q = _segnorm(y[:, :C_WIDTH], gq_ref[...], seg, segt)
    k = _segnorm(y[:, C_WIDTH:2 * C_WIDTH], gk_ref[...], seg, segt)
    v = y[:, 2 * C_WIDTH:]
    qs_ref[...] = (q * (C_HEAD_DIM ** -0.5)).astype(BF16)
    kf_ref[...] = k
    kb_ref[...] = k.astype(BF16)
    vf_ref[...] = v
    vb_ref[...] = v.astype(BF16)


def odd_in_proj(h, g, w_bf16, gq, gk):
    n = h.shape[0]
    tm = _row_tile(n, 256)
    lane = jnp.arange(C_WIDTH) // C_HEAD_DIM
    seg = (lane[:, None] == jnp.arange(128)[None, :]).astype(BF16)
    row = lambda i: (i, 0)
    fixed = lambda i: (0, 0)
    wide = pl.BlockSpec((tm, C_WIDTH), row)
    return pl.pallas_call(
        _odd_in_kernel,
        grid=(n // tm,),
        in_specs=[pl.BlockSpec((tm, D_MODEL), row),
                  pl.BlockSpec((1, D_MODEL), fixed),
                  pl.BlockSpec(w_bf16.shape, fixed),
                  pl.BlockSpec((1, C_WIDTH), fixed),
                  pl.BlockSpec((1, C_WIDTH), fixed),
                  pl.BlockSpec((C_WIDTH, 128), fixed),
                  pl.BlockSpec((128, C_WIDTH), fixed)],
        out_specs=[wide, wide, wide, wide, wide],
        out_shape=[jax.ShapeDtypeStruct((n, C_WIDTH), BF16), jax.ShapeDtypeStruct((n, C_WIDTH), F32),
                   jax.ShapeDtypeStruct((n, C_WIDTH), BF16), jax.ShapeDtypeStruct((n, C_WIDTH), F32),
                   jax.ShapeDtypeStruct((n, C_WIDTH), BF16)],
        compiler_params=_params(("parallel",)),
        name="odd_in_proj",
    )(h, g, w_bf16, gq, gk, seg, seg.T)


def _rel_bias(dist, tab_ref, h):
    nf = jnp.maximum(dist, 1).astype(F32)
    large = REL_EXACT + (jnp.log(nf / REL_EXACT) / math.log(REL_MAX_DIST / REL_EXACT)
                         * (REL_BUCKETS - REL_EXACT)).astype(I32)
    bucket = jnp.where(dist < REL_EXACT, dist, jnp.minimum(large, REL_BUCKETS - 1))
    far = tab_ref[REL_BUCKETS - 1, h]
    out = jnp.zeros(dist.shape, F32)
    for bkt in range(REL_BUCKETS - 1):
        out = jnp.where(bucket == bkt, tab_ref[bkt, h] - far, out)
    return out


def _bias_tiles_kernel(tab_ref, o_ref):
    h = pl.program_id(0)
    blk = o_ref.shape[-1]
    r = lax.broadcasted_iota(I32, (blk, blk), 0)
    c = lax.broadcasted_iota(I32, (blk, blk), 1)
    for delta in range(2):
        dist = delta * blk + r - c
        o_ref[0, delta] = jnp.where(dist >= 0, _rel_bias(jnp.maximum(dist, 0), tab_ref, h), NEG_BIG)


def bias_tiles(rel_bias, blk):
    return pl.pallas_call(
        _bias_tiles_kernel,
        grid=(C_HEADS,),
        in_specs=[pl.BlockSpec(memory_space=pltpu.SMEM)],
        out_specs=pl.BlockSpec((1, 2, blk, blk), lambda h: (h, 0, 0, 0)),
        out_shape=jax.ShapeDtypeStruct((C_HEADS, 2, blk, blk), F32),
        compiler_params=_params(("parallel",)),
        name="bias_tiles",
    )(rel_bias)


def _bias_sample_kernel(tab_ref, o_ref, *, t_new):
    rows = o_ref.shape[0]
    r = lax.broadcasted_iota(I32, (rows, PAGE_SIZE + NEW_KEYS_PAD), 0)
    c = lax.broadcasted_iota(I32, (rows, PAGE_SIZE + NEW_KEYS_PAD), 1)
    tok = lax.shift_right_logical(r, _log2(2 * C_HEADS))
    head = lax.shift_right_logical(r, 1) & (C_HEADS - 1)
    dist = jnp.where(c < PAGE_SIZE, PAGE_SIZE + tok - c, tok - (c - PAGE_SIZE))
    valid = (dist >= 0) & (c < PAGE_SIZE + t_new)
    out = jnp.zeros(dist.shape, F32)
    for h in range(C_HEADS):
        out = jnp.where(head == h, _rel_bias(jnp.maximum(dist, 0), tab_ref, h), out)
    o_ref[...] = jnp.where(valid, out, NEG_BIG)


def bias_sample(rel_bias, t_new):
    rows = t_new * C_HEADS * 2
    return pl.pallas_call(
        functools.partial(_bias_sample_kernel, t_new=t_new),
        in_specs=[pl.BlockSpec(memory_space=pltpu.SMEM)],
        out_specs=pl.BlockSpec(memory_space=pltpu.VMEM),
        out_shape=jax.ShapeDtypeStruct((rows, PAGE_SIZE + NEW_KEYS_PAD), F32),
        name="bias_sample",
    )(rel_bias)


def _softmax_update(s, v, m_ref, l_ref, acc_ref, idx):
    m_old = m_ref[idx]
    m_new = jnp.maximum(m_old, jnp.max(s, axis=-1, keepdims=True))
    alpha = jnp.exp(m_old - m_new)
    p = jnp.exp(s - m_new)
    l_ref[idx] = alpha * l_ref[idx] + jnp.sum(p, axis=-1, keepdims=True)
    acc_ref[idx] = alpha * acc_ref[idx] + _dot(p.astype(BF16), v)
    m_ref[idx] = m_new


def _subln(o, gain, lam_init):
    ms = jnp.mean(o * o, axis=-1, keepdims=True)
    return o * lax.rsqrt(ms + EPS) * gain * (1.0 - lam_init)


def _attn_prompt_kernel(lam_ref, q_ref, k_ref, v_ref, bias_ref, gs_ref, o_ref, *, lam_init):
    qi = pl.program_id(2)
    blk = q_ref.shape[1]
    q = q_ref[0]
    lane = lax.broadcasted_iota(I32, q.shape, 1)
    zero = jnp.zeros_like(q)
    qq = jnp.concatenate([jnp.where(lane < C_HEAD_DIM, q, zero), jnp.where(lane >= C_HEAD_DIM, q, zero)], axis=0)
    ones = jnp.ones((blk, C_VDIM), BF16)

    def scores(kb):
        return _dot_nt(qq, k_ref[0, pl.ds(pl.multiple_of(kb * blk, blk), blk), :])

    def update(s, kb, m, acc):
        v = v_ref[0, pl.ds(pl.multiple_of(kb * blk, blk), blk), :]
        m_new = jnp.maximum(m, jnp.max(s, axis=-1, keepdims=True))
        p = jnp.exp(s - m_new).astype(BF16)
        acc = jnp.exp(m - m_new) * acc + _dot(p, jnp.concatenate([v, ones], axis=-1))
        return m_new, acc

    def far_body(kb, carry):
        s, m, acc = carry
        s_next = scores(kb + 1)
        m, acc = update(s, kb, m, acc)
        return s_next, m, acc

    adj = jnp.maximum(qi - 1, 0)
    init = (scores(0), jnp.full((2 * blk, 1), NEG_BIG, F32), jnp.zeros((2 * blk, 2 * C_VDIM), F32))
    s_adj, m, acc = lax.fori_loop(0, adj, far_body, init)
    s_diag = scores(qi)
    def biased(s, b):
        return jnp.concatenate([s[:blk] + b, s[blk:] + b], axis=0)

    m, acc = update(biased(s_adj, bias_ref[0, 1] + jnp.where(qi == 0, NEG_BIG, 0.0)), adj, m, acc)
    m, acc = update(biased(s_diag, bias_ref[0, 0]), qi, m, acc)
    out = acc[:, :C_VDIM] / acc[:, C_VDIM:]
    o = out[:blk] - lam_ref[0] * out[blk:]
    o_ref[0] = _subln(o, gs_ref[...], lam_init)


def attn_prompt(qs, kb, vb, bias, lam, gsub, lam_init):
    b, t, _ = qs.shape
    blk = bias.shape[-1]
    return pl.pallas_call(
        functools.partial(_attn_prompt_kernel, lam_init=lam_init),
        grid=(b, C_HEADS, t // blk),
        in_specs=[pl.BlockSpec(memory_space=pltpu.SMEM),
                  pl.BlockSpec((1, blk, C_VDIM), lambda bi, h, i: (bi, i, h)),
                  pl.BlockSpec((1, t, C_VDIM), lambda bi, h, i: (bi, 0, h)),
                  pl.BlockSpec((1, t, C_VDIM), lambda bi, h, i: (bi, 0, h)),
                  pl.BlockSpec((1, 2, blk, blk), lambda bi, h, i: (h, 0, 0, 0)),
                  pl.BlockSpec((1, C_VDIM), lambda bi, h, i: (0, 0))],
        out_specs=pl.BlockSpec((1, blk, C_VDIM), lambda bi, h, i: (bi, i, h)),
        out_shape=jax.ShapeDtypeStruct((b, t, C_WIDTH), F32),
        compiler_params=_params(("parallel", "parallel", "parallel")),
        name="attn_prompt",
    )(lam, qs, kb, vb, bias, gsub)


def _attn_sample_kernel(pt_ref, lam_ref, q_ref, *refs, lam_init, pages):
    del pt_ref
    kc_refs, vc_refs = refs[:pages], refs[pages:2 * pages]
    kn_ref, vn_ref, bias_ref, gs_ref, o_ref, qe_scr, m_scr, l_scr, acc_scr = refs[2 * pages:]
    page = pl.program_id(1)
    last = pl.num_programs(1) - 1
    t_new = q_ref.shape[1]
    rows = qe_scr.shape[0]
    per_tok = 2 * C_HEADS

    @pl.when(page == 0)
    def _():
        qf = q_ref[0].astype(F32)
        seg = lax.shift_right_logical(lax.broadcasted_iota(I32, (per_tok, C_WIDTH), 1), _log2(C_HEAD_DIM))
        own = seg == lax.broadcasted_iota(I32, (per_tok, C_WIDTH), 0)
        qe = [jnp.where(own, jnp.broadcast_to(qf[t:t + 1], (per_tok, C_WIDTH)), 0.0) for t in range(t_new)]
        qe_scr[...] = jnp.concatenate(qe, axis=0).astype(BF16)
        m_scr[...] = jnp.full(m_scr.shape, NEG_BIG, F32)
        l_scr[...] = jnp.zeros(l_scr.shape, F32)
        acc_scr[...] = jnp.zeros(acc_scr.shape, F32)

    qe = qe_scr[...]
    ss = [_dot_nt(qe, kc[0].astype(BF16)) for kc in kc_refs]
    ss[-1] = ss[-1] + jnp.where(page == last, bias_ref[:, :PAGE_SIZE], 0.0)
    s = jnp.concatenate(ss, axis=-1)
    m_old = m_scr[0]
    m_new = jnp.maximum(m_old, jnp.max(s, axis=-1, keepdims=True))
    alpha = jnp.exp(m_old - m_new)
    p = jnp.exp(s - m_new)
    l_scr[0] = alpha * l_scr[0] + jnp.sum(p, axis=-1, keepdims=True)
    pb = p.astype(BF16)
    pv = _dot(pb[:, :PAGE_SIZE], vc_refs[0][0].astype(BF16))
    for i in range(1, pages):
        pv = pv + _dot(pb[:, i * PAGE_SIZE:(i + 1) * PAGE_SIZE], vc_refs[i][0].astype(BF16))
    acc_scr[0] = alpha * acc_scr[0] + pv
    m_scr[0] = m_new

    @pl.when(page == last)
    def _():
        pad = jnp.zeros((NEW_KEYS_PAD - t_new, C_WIDTH), F32)
        kn = jnp.concatenate([kn_ref[0], pad], axis=0).astype(BF16)
        vn = jnp.concatenate([vn_ref[0], pad], axis=0).astype(BF16)
        s2 = _dot_nt(qe, kn) + bias_ref[:, PAGE_SIZE:]
        _softmax_update(s2, vn, m_scr, l_scr, acc_scr, 0)
        r = lax.broadcasted_iota(I32, (rows, 1), 0)
        coef = jnp.where((r & 1) == 0, 1.0, -lam_ref[0])
        head = lax.shift_right_logical(r, 1) & (C_HEADS - 1)
        lane_head = lax.shift_right_logical(lax.broadcasted_iota(I32, (rows, C_WIDTH), 1), _log2(C_VDIM))
        x = jnp.where(lane_head == head, acc_scr[0] * (coef / l_scr[0]), 0.0)
        o = jnp.sum(x.reshape(t_new, per_tok, C_WIDTH), axis=1)
        gs = gs_ref[...]
        outs = [_subln(o[:, h * C_VDIM:(h + 1) * C_VDIM], gs, lam_init) for h in range(C_HEADS)]
        o_ref[0] = jnp.concatenate(outs, axis=-1)


def attn_sample(qs, kn, vn, cache_k, cache_v, page_table, bias, lam, gsub, lam_init):
    b, t_new, _ = qs.shape
    n_pages = page_table.shape[1]
    pages = math.gcd(n_pages, SAMPLE_PAGES_PER_STEP)
    rows = t_new * 2 * C_HEADS
    seq = lambda bi, p, pt: (bi, 0, 0)
    fixed = lambda bi, p, pt: (0, 0)

    def paged(i):
        return pl.BlockSpec((1, PAGE_SIZE, C_WIDTH), lambda bi, p, pt: (pt[bi * n_pages + p * pages + i], 0, 0))

    grid_spec = pltpu.PrefetchScalarGridSpec(
        num_scalar_prefetch=1,
        grid=(b, n_pages // pages),
        in_specs=[pl.BlockSpec(memory_space=pltpu.SMEM),
                  pl.BlockSpec((1, t_new, C_WIDTH), seq)]
                 + [paged(i) for i in range(pages)] + [paged(i) for i in range(pages)]
                 + [pl.BlockSpec((1, t_new, C_WIDTH), seq),
                  pl.BlockSpec((1, t_new, C_WIDTH), seq),
                  pl.BlockSpec((rows, PAGE_SIZE + NEW_KEYS_PAD), fixed),
                  pl.BlockSpec((1, C_VDIM), fixed)],
        out_specs=pl.BlockSpec((1, t_new, C_WIDTH), seq),
        scratch_shapes=[pltpu.VMEM((rows, C_WIDTH), BF16),
                        pltpu.VMEM((1, rows, 1), F32), pltpu.VMEM((1, rows, 1), F32),
                        pltpu.VMEM((1, rows, C_WIDTH), F32)])
    return pl.pallas_call(
        functools.partial(_attn_sample_kernel, lam_init=lam_init, pages=pages),
        grid_spec=grid_spec,
        out_shape=jax.ShapeDtypeStruct((b, t_new, C_WIDTH), F32),
        compiler_params=_params(("parallel", "arbitrary")),
        name="attn_sample",
    )(page_table.reshape(-1), lam, qs, *([cache_k] * pages), *([cache_v] * pages), kn, vn, bias, gsub)


def _col_max(x):
    return jnp.max(x, axis=0, keepdims=True)


def _col_min(x):
    return jnp.min(x, axis=0, keepdims=True)


def _extract_top(work, order, k, want_rank):
    rows = work.shape[0]
    rank = jnp.full(work.shape, float(k), F32) if want_rank else None
    tops, picks = [], []
    for i in range(k):
        m = _col_max(work)
        pick = _col_min(jnp.where(work == m, order, float(rows * rows)))
        chosen = order == pick
        if want_rank:
            rank = jnp.where(chosen, float(i), rank)
        work = jnp.where(chosen, -jnp.inf, work)
        tops.append(m)
        picks.append(pick)
    return tops, picks, rank


def _stack_rows(rows):
    k = len(rows)
    krow = lax.broadcasted_iota(I32, (k, rows[0].shape[1]), 0)
    out = jnp.zeros((k, rows[0].shape[1]), F32)
    for i, r in enumerate(rows):
        out = jnp.where(krow == i, r, out)
    return out


def _peer_prep_kernel(h_ref, g_ref, wq_ref, keys_ref, xn_ref, r2_ref, e2_ref, cnt_ref, c_ref, s_scr):
    K = PEER_TOPK
    xn = _rms(h_ref[...], g_ref[...]).astype(BF16)
    xn_ref[...] = xn
    tn = xn.shape[0]
    key_id = lax.broadcasted_iota(I32, (N_KEYS, 128), 0).astype(F32)
    ci = lax.broadcasted_iota(I32, (PEER_CAND, 128), 0)
    mid = ci - K
    cand_p = jnp.where(ci < K, 0, jnp.where(ci < PEER_CAND - 8, 1 + lax.shift_right_logical(mid, 3), ci - (PEER_CAND - 16)))
    cand_q = jnp.where(ci < K, ci, jnp.where(ci < PEER_CAND - 8, mid & 7, 0))
    cand_pos = (cand_p * K + cand_q).astype(F32)
    prow = lax.broadcasted_iota(I32, (8, 128), 0)

    def pair_table(a, b, op):
        blocks = [op(a[0:1], b)] + [op(a[p:p + 1], b[0:8]) for p in range(1, 8)] + [op(a[8:16], b[0:1])]
        return jnp.concatenate(blocks, axis=0)

    def head_body(h, carry):
        w = wq_ref[pl.ds(pl.multiple_of(h * PEER_QDIM, PEER_QDIM), PEER_QDIM), :]
        qt = _dot_nt(w, xn).astype(BF16)
        s_scr[0] = _dot(keys_ref[h, 0], qt[:N_KEYS])
        s_scr[1] = _dot(keys_ref[h, 1], qt[N_KEYS:])
        for ct in range(tn // 128):
            cs = slice(ct * 128, (ct + 1) * 128)
            s1 = s_scr[0, :, cs]
            s2 = s_scr[1, :, cs]
            tops1, picks1, _ = _extract_top(s1, key_id, K, False)
            tops2, _, r2 = _extract_top(s2, key_id, K, True)
            top1 = _stack_rows(tops1)
            top2 = _stack_rows(tops2)
            _, _, sel_rank = _extract_top(pair_table(top1, top2, jnp.add), cand_pos, K, True)
            sel = sel_rank < K
            e1 = jnp.exp(top1 - tops1[0])
            e2 = jnp.exp(top2 - tops2[0])
            z = jnp.sum(jnp.where(sel, pair_table(e1, e2, jnp.multiply), 0.0), axis=0, keepdims=True)
            self32 = sel.astype(F32)
            cnt_lo = jnp.zeros((8, 128), F32)
            cnt_lo = jnp.where(prow == 0, jnp.sum(self32[0:K], axis=0, keepdims=True), cnt_lo)
            for p in range(1, 8):
                blk = self32[K + 8 * (p - 1):K + 8 * p]
                cnt_lo = jnp.where(prow == p, jnp.sum(blk, axis=0, keepdims=True), cnt_lo)
            cnt16 = jnp.concatenate([cnt_lo, self32[PEER_CAND - 8:]], axis=0)
            cnt_a = jnp.zeros((N_KEYS, 128), F32)
            for p in range(K):
                cnt_a = jnp.where(key_id == picks1[p], cnt16[p:p + 1], cnt_a)
            r2_ref[h, :, cs] = r2.astype(r2_ref.dtype)
            e2_ref[h, :, cs] = jnp.exp(s2 - tops2[0]).astype(e2_ref.dtype)
            cnt_ref[h, :, cs] = cnt_a.astype(cnt_ref.dtype)
            c_ref[h, :, cs] = (jnp.exp(s1 - tops1[0]) / z).astype(c_ref.dtype)
        return carry

    lax.fori_loop(0, PEER_HEADS, head_body, 0)


def peer_prep(h, g, wq_t, keys):
    n = h.shape[0]
    tn = _row_tile(n, 256)
    tab = jax.ShapeDtypeStruct((PEER_HEADS, N_KEYS, n), jnp.uint32)
    tab_bf = jax.ShapeDtypeStruct((PEER_HEADS, N_KEYS, n), BF16)
    tspec = pl.BlockSpec((PEER_HEADS, N_KEYS, tn), lambda i: (0, 0, i))
    return pl.pallas_call(
        _peer_prep_kernel,
        grid=(n // tn,),
        in_specs=[pl.BlockSpec((tn, D_MODEL), lambda i: (i, 0)),
                  pl.BlockSpec((1, D_MODEL), lambda i: (0, 0)),
                  pl.BlockSpec(wq_t.shape, lambda i: (0, 0)),
                  pl.BlockSpec(keys.shape, lambda i: (0, 0, 0, 0))],
        out_specs=[pl.BlockSpec((tn, D_MODEL), lambda i: (i, 0)), tspec, tspec, tspec, tspec],
        out_shape=[jax.ShapeDtypeStruct((n, D_MODEL), BF16), tab_bf, tab_bf, tab_bf, tab_bf],
        scratch_shapes=[pltpu.VMEM((2, N_KEYS, tn), F32)],
        compiler_params=_params(("parallel",)),
        name="peer_prep",
    )(h, g, wq_t, keys)


def _dup_bf16(x):
    w = lax.bitcast_convert_type(x.astype(BF16).astype(F32), jnp.uint32)
    return w | lax.shift_right_logical(w, jnp.uint32(16))


def _rows_bf16(word_row, rows):
    return pltpu.bitcast(jnp.broadcast_to(word_row, (rows // 2, word_row.shape[1])), BF16)


def _peer_dense_kernel(xn_ref, u_ref, vt_ref, r2_ref, e2_ref, cnt_ref, c_ref, o_ref, act_scr, w0_scr, w1_scr):
    j = pl.program_id(1)
    tn = act_scr.shape[1]

    @pl.when(j == 0)
    def _():
        o_ref[...] = jnp.zeros(o_ref.shape, F32)
        w1_scr[...] = jnp.zeros(w1_scr.shape, BF16)

    def step(w_prev, w_cur):
        o_ref[...] += _dot(vt_ref[...], w_prev[...])
        act_scr[...] = _dot_nt(u_ref[...], xn_ref[...])
        for ct in range(tn // PEER_LANES):
            cs = slice(ct * PEER_LANES, (ct + 1) * PEER_LANES)
            for b0 in range(0, N_KEYS, PEER_B_ROWS):
                bs = slice(b0, b0 + PEER_B_ROWS)
                for i0 in range(0, PEER_A_PER_STEP, PEER_A_INNER):
                    gates = [None] * PEER_A_INNER
                    for h in range(PEER_HEADS):
                        r2 = r2_ref[h, bs, cs]
                        e2 = e2_ref[h, bs, cs]
                        for ii in range(PEER_A_INNER):
                            i = i0 + ii
                            cnt = jnp.broadcast_to(cnt_ref[h, i:i + 1, cs], r2.shape)
                            c = jnp.broadcast_to(c_ref[h, i:i + 1, cs], r2.shape)
                            term = jnp.where(r2 < cnt, e2, jnp.zeros_like(e2)) * c
                            gates[ii] = term if gates[ii] is None else gates[ii] + term
                    for ii in range(PEER_A_INNER):
                        rs = slice((i0 + ii) * N_KEYS + b0, (i0 + ii) * N_KEYS + b0 + PEER_B_ROWS)
                        w_cur[rs, cs] = gates[ii] * _gelu_tanh(act_scr[rs, cs]).astype(BF16)

    @pl.when(j % 2 == 0)
    def _():
        step(w1_scr, w0_scr)

    @pl.when(j % 2 == 1)
    def _():
        step(w0_scr, w1_scr)


def peer_dense(xn, u_bf16, vt_bf16, r2, e2, cnt, c):
    n = xn.shape[0]
    tn = _row_tile(n, 512)
    ne = PEER_A_PER_STEP * N_KEYS
    nb = N_KEYS // PEER_A_PER_STEP
    cur = lambda j: jnp.minimum(j, nb - 1)
    tspec = pl.BlockSpec((PEER_HEADS, N_KEYS, tn), lambda i, j: (0, 0, i))
    aspec = pl.BlockSpec((PEER_HEADS, PEER_A_PER_STEP, tn), lambda i, j: (0, cur(j), i))
    return pl.pallas_call(
        _peer_dense_kernel,
        grid=(n // tn, nb + 1),
        in_specs=[pl.BlockSpec((tn, D_MODEL), lambda i, j: (i, 0)),
                  pl.BlockSpec((ne, D_MODEL), lambda i, j: (cur(j), 0)),
                  pl.BlockSpec((D_MODEL, ne), lambda i, j: (0, jnp.maximum(j - 1, 0))),
                  tspec, tspec, aspec, aspec],
        out_specs=pl.BlockSpec((D_MODEL, tn), lambda i, j: (0, i)),
        out_shape=jax.ShapeDtypeStruct((D_MODEL, n), F32),
        scratch_shapes=[pltpu.VMEM((ne, tn), F32), pltpu.VMEM((ne, tn), BF16), pltpu.VMEM((ne, tn), BF16)],
        compiler_params=_params(("parallel", "arbitrary")),
        name="peer_dense",
    )(xn, u_bf16, vt_bf16, r2, e2, cnt, c)


def _row(x):
    return x.astype(F32).reshape(1, -1)


def _run_group(x, p, s5_re0, s5_im0, hgrn0, cache_k, cache_v, page_table, w):
    batch, seqlen, _ = x.shape
    n = batch * seqlen
    depth = p.shape[0]
    h = x.reshape(n, D_MODEL)
    s5_re, s5_im, hg, ks, vs = [], [], [], [], []
    for layer in range(depth):
        j = layer // 2
        if layer % 2 == 0:
            u4, rest = even_in_proj(h, _row(w['norm_mix'][layer]), w['w_in_ab'][j])
            y4, hr, hi = s5_mixer(u4, w['s5_tabs'][j], s5_re0[j], s5_im0[j], batch)
            yb, st = hgrn_mixer(rest, w['hgrn_lb'][j], w['hgrn_ng'][j],
                                jnp.swapaxes(hgrn0[j], -1, -2), batch, seqlen)
            h = even_out_proj(y4, u4, _row(w['s5_d'][j]), w['s5_glu_w'][j], _row(w['s5_glu_b'][j]),
                              yb, w['w_out_ab'][j], h)
            s5_re.append(hr)
            s5_im.append(hi)
            hg.append(jnp.swapaxes(st, -1, -2))
        else:
            qs, kf, kb, vf, vb = odd_in_proj(h, _row(w['norm_mix'][layer]), w['w_in_c'][j],
                                             w['c_gq'][j], w['c_gk'][j])
            lam_init = 0.8 - 0.6 * math.exp(-0.3 * layer)
            lam = w['c_lam'][j] + lam_init
            gsub = _row(w['c_subln'][j])
            shape3 = (batch, seqlen, C_WIDTH)
            if page_table is None:
                o = attn_prompt(qs.reshape(shape3), kb.reshape(shape3), vb.reshape(shape3),
                                w['bias_tiles'], lam, gsub, lam_init)
            else:
                ck = cache_k.reshape(-1, PAGE_SIZE, C_WIDTH)
                cv = cache_v.reshape(-1, PAGE_SIZE, C_WIDTH)
                o = attn_sample(qs.reshape(shape3), kf.reshape(shape3), vf.reshape(shape3), ck, cv,
                                page_table + j * cache_k.shape[1], w['bias_sample'], lam, gsub, lam_init)
            h = residual_proj(o.reshape(n, C_WIDTH), w['w_out_c'][j], h)
            ks.append(kf.reshape(batch, seqlen, C_HEADS, 2 * C_HEAD_DIM))
            vs.append(vf.reshape(batch, seqlen, C_HEADS, C_VDIM))
        xn, r2, e2, cnt, c = peer_prep(h, _row(w['norm_ffn'][layer]), w['peer_wq_t'][layer], w['peer_keys'][layer])
        peer_t = peer_dense(xn, w['peer_u'][layer], w['peer_v_t'][layer], r2, e2, cnt, c)
        h = ple_step(h, peer_t, p[layer].reshape(n, -1), _row(w['norm_ple'][layer]),
                     w['ple_gate'][layer], w['ple_proj'][layer])
    return (h.reshape(batch, seqlen, D_MODEL), jnp.stack(s5_re), jnp.stack(s5_im), jnp.stack(hg),
            jnp.stack(ks), jnp.stack(vs))


def kernel(x_prompt, x_sample, p_prompt, p_sample, state_s5_re, state_s5_im, state_hgrn, cache_k, cache_v,
           page_table, norm_mix, norm_ffn, norm_ple, w_in_ab, w_out_ab, s5_a_re, s5_a_im, s5_log_dt,
           s5_b_re, s5_b_im, s5_c_re, s5_c_im, s5_d, s5_glu_w, s5_glu_b, hgrn_lower_bounds, hgrn_norm,
           w_in_c, w_out_c, c_q_norm, c_k_norm, c_lam_q1, c_lam_k1, c_lam_q2, c_lam_k2, c_subln, rel_bias,
           peer_wq, peer_keys, peer_u, peer_v, ple_proj, ple_gate):
    n_even = w_in_ab.shape[0]
    n_odd = w_in_c.shape[0]
    t_new = x_sample.shape[1]
    lb_all = jnp.cumsum(jax.nn.softmax(hgrn_lower_bounds.astype(F32), axis=0), axis=0)
    tile_heads = lambda g, reps: jnp.tile(g.astype(F32).reshape(-1), reps).reshape(1, -1)
    lam_dot = lambda a, b: jnp.exp(jnp.sum(a.astype(F32) * b.astype(F32), axis=-1))
    w = dict(
        norm_mix=norm_mix, norm_ffn=norm_ffn, norm_ple=norm_ple,
        w_in_ab=w_in_ab.astype(BF16), w_out_ab=w_out_ab.astype(BF16),
        s5_tabs=[s5_tables(s5_a_re[j], s5_a_im[j], s5_log_dt[j], s5_b_re[j], s5_b_im[j], s5_c_re[j], s5_c_im[j])
                 for j in range(n_even)],
        s5_d=s5_d, s5_glu_w=s5_glu_w.astype(BF16), s5_glu_b=s5_glu_b,
        hgrn_lb=[lb_all[j].reshape(1, -1) for j in range(n_even)],
        hgrn_ng=[tile_heads(hgrn_norm[j], B_HEADS) for j in range(n_even)],
        w_in_c=w_in_c.astype(BF16), w_out_c=w_out_c.astype(BF16),
        c_gq=[tile_heads(c_q_norm[j], C_HEADS) for j in range(n_odd)],
        c_gk=[tile_heads(c_k_norm[j], C_HEADS) for j in range(n_odd)],
        c_lam=[(lam_dot(c_lam_q1[j], c_lam_k1[j]) - lam_dot(c_lam_q2[j], c_lam_k2[j])).reshape(1)
               for j in range(n_odd)],
        c_subln=c_subln,
        bias_tiles=bias_tiles(rel_bias.astype(F32), ATT_BLOCK),
        bias_sample=bias_sample(rel_bias.astype(F32), t_new),
        peer_wq_t=jnp.swapaxes(peer_wq, -1, -2).astype(BF16),
        peer_keys=peer_keys.astype(BF16),
        peer_u=peer_u.astype(BF16),
        peer_v_t=jnp.swapaxes(peer_v, -1, -2).astype(BF16),
        ple_proj=ple_proj.astype(BF16), ple_gate=ple_gate.astype(BF16),
    )
    bp = x_prompt.shape[0]
    z_s5 = jnp.zeros((n_even, bp, A_GROUPS, A_STATE), F32)
    z_hg = jnp.zeros((n_even, bp, B_HEADS, B_DK, B_DK), F32)
    y_p, s5r_p, s5i_p, hg_p, k_p, v_p = _run_group(x_prompt, p_prompt, z_s5, z_s5, z_hg, None, None, None, w)
    y_s, s5r_s, s5i_s, hg_s, k_s, v_s = _run_group(x_sample, p_sample, state_s5_re, state_s5_im, state_hgrn,
                                                   cache_k, cache_v, page_table, w)
    return (y_p, y_s, s5r_p, s5i_p, hg_p, k_p, v_p, s5r_s, s5i_s, hg_s, k_s, v_s)
```
